```python
import math
import jax, jax.numpy as jnp
from jax import lax
import numpy as np

D_MODEL = 1024
BATCH = 8
SEQ = 2048
DEPTH = 4

GRID_W = 64
CTX_LEN = 256
N_MIXERS = 2
DA_HEADS = 8
DA_HEAD_DIM = D_MODEL // DA_HEADS // 2
DA_V_DIM = 2 * DA_HEAD_DIM
Q_BLOCK = 128
ROPE_THETA = 10000.0
CHUNK = 128
SG_WIDTH = D_MODEL
SG_GROUPS = 8
SG_GROUP_DIM = SG_WIDTH // SG_GROUPS
D_FF = 2816
N_MOD = 9
RMS_EPS = 1e-6
LN_EPS = 1e-5
N_A_LAYERS = (DEPTH + 1) // 2
N_B_LAYERS = DEPTH // 2

kernel_name = "hybrid_diffattn_sgmlp_macaron_ctxprefix"


def rms_norm(x, g, eps=RMS_EPS):
    xf = x.astype(jnp.float32)
    y = xf * lax.rsqrt(jnp.mean(xf * xf, axis=-1, keepdims=True) + eps)
    return (y * g.astype(jnp.float32)).astype(x.dtype)


def layer_norm(x, g, b, eps=LN_EPS):
    xf = x.astype(jnp.float32)
    mu = jnp.mean(xf, axis=-1, keepdims=True)
    xc = xf - mu
    y = xc * lax.rsqrt(jnp.mean(xc * xc, axis=-1, keepdims=True) + eps)
    return (y * g.astype(jnp.float32) + b.astype(jnp.float32)).astype(x.dtype)


def modulate(x, g, shift, scale):
    return rms_norm(x, g) * (1 + scale) + shift


def ffn_sublayer(h, shift, scale, gate, g, w_gu, w_down):
    xn = modulate(h, g, shift, scale)
    gu = xn @ w_gu
    a, u = gu[..., :D_FF], gu[..., D_FF:]
    y = (jax.nn.silu(a) * u) @ w_down
    return h + 0.5 * gate * y


def axial_rope_tables(n_tokens, dtype):
    rows_n = n_tokens // GRID_W
    row = jnp.repeat(jnp.arange(rows_n), GRID_W)
    col = jnp.tile(jnp.arange(GRID_W), rows_n)
    n_freq = DA_HEAD_DIM // 4
    inv = ROPE_THETA ** (-jnp.arange(n_freq, dtype=jnp.float32) / n_freq)
    pos = jnp.stack([row, col], axis=-1).astype(jnp.float32)
    ang = pos[:, :, None] * inv
    return jnp.cos(ang).astype(dtype), jnp.sin(ang).astype(dtype)


def apply_axial_rope(x, cos, sin):
    shp = x.shape
    xr = x.reshape(shp[:-1] + (2, 2, DA_HEAD_DIM // 4))
    x1, x2 = xr[..., 0, :], xr[..., 1, :]
    cb, sb = cos[:, None, None], sin[:, None, None]
    out = jnp.stack([x1 * cb - x2 * sb, x2 * cb + x1 * sb], axis=-2)
    return out.reshape(shp)


def diff_attend(q, k, v, lam):
    s = jnp.einsum('bqhrd,bkhrd->bhrqk', q, k,
                   preferred_element_type=jnp.float32) * (DA_HEAD_DIM ** -0.5)
    p = jax.nn.softmax(s, axis=-1)
    w = p[:, :, 0] - lam * p[:, :, 1]
    return jnp.einsum('bhqk,bkhe->bqhe', w.astype(v.dtype), v)


def diff_head_out(o, subln_g, lam_init, w_out):
    o = rms_norm(o, subln_g) * (1.0 - lam_init)
    return o.reshape(o.shape[:2] + (D_MODEL,)) @ w_out


def diff_attention_mixer(xl, xc, w_in, w_out, lam_vecs, subln_g, lam_init, cos, sin, ctx_out):
    B, S, _ = xl.shape
    C = xc.shape[1]
    D = D_MODEL
    lv = lam_vecs.astype(jnp.float32)
    lam = (jnp.exp(jnp.sum(lv[0] * lv[1])) - jnp.exp(jnp.sum(lv[2] * lv[3])) + lam_init)
    qkv = xl @ w_in
    q = apply_axial_rope(qkv[..., :D].reshape(B, S, DA_HEADS, 2, DA_HEAD_DIM), cos, sin)
    k = apply_axial_rope(qkv[..., D:2 * D].reshape(B, S, DA_HEADS, 2, DA_HEAD_DIM), cos, sin)
    v = qkv[..., 2 * D:].reshape(B, S, DA_HEADS, DA_V_DIM)
    kv_c = xc @ w_in[:, D:]
    k_c = kv_c[..., :D].reshape(B, C, DA_HEADS, 2, DA_HEAD_DIM)
    v_c = kv_c[..., D:].reshape(B, C, DA_HEADS, DA_V_DIM)
    k_all = jnp.concatenate([k_c, k], axis=1)
    v_all = jnp.concatenate([v_c, v], axis=1)
    nb = S // Q_BLOCK
    qb = q.reshape(B, nb, Q_BLOCK, DA_HEADS, 2, DA_HEAD_DIM).transpose(1, 0, 2, 3, 4, 5)
    ob = lax.map(lambda qi: diff_attend(qi, k_all, v_all, lam), qb)
    o = ob.transpose(1, 0, 2, 3, 4).reshape(B, S, DA_HEADS, DA_V_DIM)
    y = diff_head_out(o, subln_g, lam_init, w_out)
    if ctx_out:
        q_c = (xc @ w_in[:, :D]).reshape(B, C, DA_HEADS, 2, DA_HEAD_DIM)
        o_c = diff_attend(q_c, k_c, v_c, lam)
        y_c = diff_head_out(o_c, subln_g, lam_init, w_out)
    else:
        y_c = None
    return y, y_c


def spatial_gating_mlp(x, w_in, ln_g, ln_b, w_s, b_s, w_out):
    B, L, _ = x.shape
    z = jax.nn.gelu(x @ w_in)
    u, v = z[..., :SG_WIDTH], z[..., SG_WIDTH:]
    v = layer_norm(v, ln_g, ln_b)
    nc = L // CHUNK
    vg = v.reshape(B, nc, CHUNK, SG_GROUPS, SG_GROUP_DIM)
    mixed = jnp.einsum('gts,bnsgc->bntgc', w_s, vg) + b_s.T[None, None, :, :, None]
    return (u * mixed.reshape(B, L, SG_WIDTH)) @ w_out


def setup_inputs(seed: int = 0) -> dict:
    key = jax.random.key(seed)
    ks = jax.random.split(key, 24)
    f32 = jnp.float32
    D = D_MODEL
    nrm = lambda k, shp, s: jax.random.normal(k, shp, f32) * s
    return {
        "x": nrm(ks[0], (BATCH, SEQ, D), 1.0),
        "c": nrm(ks[1], (BATCH, D), 1.0),
        "ctx": nrm(ks[2], (BATCH, CTX_LEN, D), 1.0),
        "c_ctx": nrm(ks[3], (D,), 1.0),
        "w_mod": nrm(ks[4], (DEPTH, D, N_MOD * D), 0.5 * D ** -0.5),
        "b_mod": nrm(ks[5], (DEPTH, N_MOD * D), 0.01),
        "norm_g": 1.0 + nrm(ks[6], (DEPTH, 3, D), 0.01),
        "w_ffn_gu": nrm(ks[7], (DEPTH, 2, D, 2 * D_FF), D ** -0.5),
        "w_ffn_down": nrm(ks[8], (DEPTH, 2, D_FF, D), D_FF ** -0.5),
        "da_w_in": nrm(ks[9], (N_A_LAYERS, D, 3 * D), D ** -0.5),
        "da_w_out": nrm(ks[10], (N_A_LAYERS, D, D), D ** -0.5),
        "da_lambda": nrm(ks[11], (N_A_LAYERS, 4, DA_HEAD_DIM), 0.1),
        "da_subln_g": 1.0 + nrm(ks[12], (N_A_LAYERS, DA_V_DIM), 0.01),
        "sg_w_in": nrm(ks[13], (N_B_LAYERS, D, 2 * SG_WIDTH), D ** -0.5),
        "sg_ln_g": 1.0 + nrm(ks[14], (N_B_LAYERS, SG_WIDTH), 0.01),
        "sg_ln_b": nrm(ks[15], (N_B_LAYERS, SG_WIDTH), 0.01),
        "sg_w_s": nrm(ks[16], (N_B_LAYERS, SG_GROUPS, CHUNK, CHUNK), CHUNK ** -0.5),
        "sg_b_s": 1.0 + nrm(ks[17], (N_B_LAYERS, SG_GROUPS, CHUNK), 0.01),
        "sg_w_out": nrm(ks[18], (N_B_LAYERS, SG_WIDTH, D), SG_WIDTH ** -0.5),
        "final_g": 1.0 + nrm(ks[19], (D,), 0.01),
    }


def reference(x, c, ctx, c_ctx, w_mod, b_mod, norm_g, w_ffn_gu, w_ffn_down,
              da_w_in, da_w_out, da_lambda, da_subln_g,
              sg_w_in, sg_ln_g, sg_ln_b, sg_w_s, sg_b_s, sg_w_out, final_g):
    B, S, _ = x.shape
    cos, sin = axial_rope_tables(S, x.dtype)
    last_ctx_layer = max(i for i in range(DEPTH) if i % N_MIXERS == 0)
    sc = jax.nn.silu(c)
    scc = jax.nn.silu(c_ctx)
    h, hc = x, ctx
    for i in range(DEPTH):
        j = i // N_MIXERS
        mode = 'full' if i < last_ctx_layer else ('kv' if i == last_ctx_layer else 'none')
        mx = (sc @ w_mod[i] + b_mod[i]).reshape(B, N_MOD, 1, D_MODEL)
        mx = [mx[:, k] for k in range(N_MOD)]
        mc = None
        if mode != 'none':
            mc = (scc @ w_mod[i] + b_mod[i]).reshape(N_MOD, D_MODEL)
        h = ffn_sublayer(h, mx[0], mx[1], mx[2], norm_g[i, 0], w_ffn_gu[i, 0], w_ffn_down[i, 0])
        if mode != 'none':
            hc = ffn_sublayer(hc, mc[0], mc[1], mc[2], norm_g[i, 0], w_ffn_gu[i, 0], w_ffn_down[i, 0])
        xn = modulate(h, norm_g[i, 1], mx[3], mx[4])
        xcn = modulate(hc, norm_g[i, 1], mc[3], mc[4]) if mode != 'none' else None
        if i % N_MIXERS == 0:
            lam_init = 0.8 - 0.6 * math.exp(-0.3 * i)
            y, yc = diff_attention_mixer(xn, xcn, da_w_in[j], da_w_out[j], da_lambda[j],
                                         da_subln_g[j], lam_init, cos, sin, mode == 'full')
        else:
            y = spatial_gating_mlp(xn, sg_w_in[j], sg_ln_g[j], sg_ln_b[j], sg_w_s[j], sg_b_s[j], sg_w_out[j])
            yc = (spatial_gating_mlp(xcn, sg_w_in[j], sg_ln_g[j], sg_ln_b[j], sg_w_s[j], sg_b_s[j], sg_w_out[j])
                  if mode == 'full' else None)
        h = h + mx[5] * y
        if mode == 'full':
            hc = hc + mc[5] * yc
        h = ffn_sublayer(h, mx[6], mx[7], mx[8], norm_g[i, 2], w_ffn_gu[i, 1], w_ffn_down[i, 1])
        if mode == 'full':
            hc = ffn_sublayer(hc, mc[6], mc[7], mc[8], norm_g[i, 2], w_ffn_gu[i, 1], w_ffn_down[i, 1])
    return rms_norm(h, final_g)
```

```python
import functools
import math

import jax
import jax.numpy as jnp
from jax import lax
from jax.experimental import pallas as pl
from jax.experimental.pallas import tpu as pltpu

F32 = jnp.float32
BF16 = jnp.bfloat16

GRID_W = 64
N_MIXERS = 2
N_HEADS = 8
HEAD_DIM = 64
V_DIM = 2 * HEAD_DIM
ROPE_THETA = 10000.0
CHUNK = 128
SG_GROUPS = 8
N_MOD = 9
RMS_EPS = 1e-6
LN_EPS = 1e-5

V7X_LANES = 128
V7X_VMEM_BYTES = 64 * 1024 * 1024
VMEM_LIMIT_BYTES = V7X_VMEM_BYTES - 8 * 1024 * 1024

MOD_ROWS = 16
ATTN_TILE = 256
ROW_TILE = 512


def _params(n_axes):
    return pltpu.CompilerParams(dimension_semantics=("arbitrary",) * n_axes,
                                vmem_limit_bytes=VMEM_LIMIT_BYTES)


def _rms(x, g):
    y = x * lax.rsqrt(jnp.mean(x * x, axis=-1, keepdims=True) + RMS_EPS)
    return y * g


def _modulated_bf16(h, g, shift, scale):
    return (_rms(h, g) * (1.0 + scale) + shift).astype(BF16)


def _stacked(arr, idx, single_buffer=False):
    tail = arr.shape[len(idx):]
    index = tuple(idx) + (0,) * len(tail)
    kwargs = dict(pipeline_mode=pl.Buffered(1)) if single_buffer else {}
    return pl.BlockSpec((None,) * len(idx) + tail, lambda *_: index, **kwargs)


def _mod_spec(layer, k, d, row_of_tile):
    return pl.BlockSpec((None, None, None, 1, d), lambda t: (layer, row_of_tile(t), k, 0, 0))


def _mod_kernel(c_ref, w_ref, b_ref, o_ref):
    sc = jax.nn.silu(c_ref[...]).astype(BF16)
    o_ref[...] = jnp.dot(sc, w_ref[...].astype(BF16), preferred_element_type=F32) + b_ref[...]


def _mod_table(c_all, w_mod, b_mod):
    depth, d, nd = w_mod.shape
    tn = d
    return pl.pallas_call(
        _mod_kernel,
        grid=(depth, nd // tn),
        in_specs=[
            pl.BlockSpec((MOD_ROWS, d), lambda i, n: (0, 0)),
            pl.BlockSpec((None, d, tn), lambda i, n: (i, 0, n)),
            pl.BlockSpec((None, 1, tn), lambda i, n: (i, 0, n)),
        ],
        out_specs=pl.BlockSpec((None, MOD_ROWS, tn), lambda i, n: (i, 0, n)),
        out_shape=jax.ShapeDtypeStruct((depth, MOD_ROWS, nd), F32),
        compiler_params=_params(2),
        name="mod_table",
    )(c_all, w_mod, b_mod.reshape(depth, 1, nd))


def _ffn_kernel(h_ref, sh_ref, sc_ref, gt_ref, g_ref, wgu_ref, wd_ref, *rest, n_chunks, final):
    if final:
        fg_ref, o_ref = rest
    else:
        (o_ref,) = rest
    d_ff = wd_ref.shape[0]
    h = h_ref[...]
    xb = _modulated_bf16(h, g_ref[...], sh_ref[...], sc_ref[...])
    fc = d_ff // n_chunks
    y = None
    for c in range(n_chunks):
        a = jnp.dot(xb, wgu_ref[:, c * fc:(c + 1) * fc], preferred_element_type=F32)
        u = jnp.dot(xb, wgu_ref[:, d_ff + c * fc:d_ff + (c + 1) * fc], preferred_element_type=F32)
        act = (jax.nn.silu(a) * u).astype(BF16)
        p = jnp.dot(act, wd_ref[c * fc:(c + 1) * fc, :], preferred_element_type=F32)
        y = p if y is None else y + p
    out = h + (0.5 * gt_ref[...]) * y
    if final:
        out = _rms(out, fg_ref[...])
    o_ref[...] = out


def _ffn(h, mods, norm_g, w_gu, w_down, layer, which, n_rows, row_of_tile, final_g=None):
    d = h.shape[1]
    tm = ROW_TILE
    k0 = 6 * which
    final = final_g is not None
    in_specs = [
        pl.BlockSpec((tm, d), lambda t: (t, 0)),
        _mod_spec(layer, k0, d, row_of_tile),
        _mod_spec(layer, k0 + 1, d, row_of_tile),
        _mod_spec(layer, k0 + 2, d, row_of_tile),
        _stacked(norm_g, (layer, 2 * which)),
        _stacked(w_gu, (layer, which), single_buffer=True),
        _stacked(w_down, (layer, which), single_buffer=True),
    ]
    args = [h, mods, mods, mods, norm_g, w_gu, w_down]
    if final:
        in_specs.append(pl.BlockSpec((1, d), lambda t: (0, 0)))
        args.append(final_g.reshape(1, d))
    return pl.pallas_call(
        functools.partial(_ffn_kernel, n_chunks=2, final=final),
        grid=(n_rows // tm,),
        in_specs=in_specs,
        out_specs=pl.BlockSpec((tm, d), lambda t: (t, 0)),
        out_shape=jax.ShapeDtypeStruct((n_rows, d), F32),
        compiler_params=_params(1),
        name="ffn",
    )(*args)


def _qkv_kernel(h_ref, sh_ref, sc_ref, g_ref, w_ref, cos_ref, sin_ref, q_ref, k_ref, v_ref):
    d = h_ref.shape[1]
    xb = _modulated_bf16(h_ref[...], g_ref[...], sh_ref[...], sc_ref[...])
    cos = cos_ref[...]
    sin = sin_ref[...]
    lane = lax.broadcasted_iota(jnp.int32, cos.shape, 1)
    first_half = (lane % (HEAD_DIM // 2)) < (HEAD_DIM // 4)

    def rope(x):
        partner = jnp.where(first_half,
                            pltpu.roll(x, V7X_LANES - HEAD_DIM // 4, axis=1),
                            pltpu.roll(x, HEAD_DIM // 4, axis=1))
        return x * cos + partner * sin

    q = jnp.dot(xb, w_ref[:, :d], preferred_element_type=F32)
    k = jnp.dot(xb, w_ref[:, d:2 * d], preferred_element_type=F32)
    v = jnp.dot(xb, w_ref[:, 2 * d:], preferred_element_type=F32)
    for hd in range(N_HEADS):
        sl = slice(hd * V_DIM, (hd + 1) * V_DIM)
        q_ref[hd] = (rope(q[:, sl]) * (HEAD_DIM ** -0.5)).astype(BF16)
        k_ref[hd] = rope(k[:, sl]).astype(BF16)
        v_ref[hd] = v[:, sl].astype(BF16)


def _tile_maps(dims):
    b, s, c = dims
    tm = ATTN_TILE
    n_lat = b * s // tm
    per_b = s // tm
    blocks_per_b = (s + c) // tm

    def row_of_tile(t):
        return jnp.minimum(t // per_b, b)

    def rope_block(t):
        return jnp.where(t < n_lat, t % per_b, per_b)

    def hm_block(t):
        return jnp.where(t < n_lat, (t // per_b) * blocks_per_b + 1 + t % per_b, (t - n_lat) * blocks_per_b)

    return row_of_tile, rope_block, hm_block


def _qkv(h, mods, norm_g, w_in, cos_t, sin_t, layer, j, dims):
    b, s, c = dims
    d = h.shape[1]
    tm = ATTN_TILE
    row_of_tile, rope_block, hm_block = _tile_maps(dims)
    hm = jax.ShapeDtypeStruct((N_HEADS, b * (s + c), V_DIM), BF16)
    hm_spec = pl.BlockSpec((N_HEADS, tm, V_DIM), lambda t: (0, hm_block(t), 0))
    return pl.pallas_call(
        _qkv_kernel,
        grid=(b * (s + c) // tm,),
        in_specs=[
            pl.BlockSpec((tm, d), lambda t: (t, 0)),
            _mod_spec(layer, 3, d, row_of_tile),
            _mod_spec(layer, 4, d, row_of_tile),
            _stacked(norm_g, (layer, 1)),
            _stacked(w_in, (j,), single_buffer=True),
            pl.BlockSpec((tm, V_DIM), lambda t: (rope_block(t), 0)),
            pl.BlockSpec((tm, V_DIM), lambda t: (rope_block(t), 0)),
        ],
        out_specs=[hm_spec, hm_spec, hm_spec],
        out_shape=[hm, hm, hm],
        compiler_params=_params(1),
        name="qkv_rope",
    )(h, mods, mods, norm_g, w_in, cos_t, sin_t)


def _attn_kernel(q_ref, k_ref, v_ref, lam_ref, sg_ref, *rest, lam_init):
    o_ref = rest[-1]
    tq = q_ref.shape[0]
    q = q_ref[...]
    lane = lax.broadcasted_iota(jnp.int32, q.shape, 1)
    zero = jnp.zeros_like(q)
    qs = jnp.concatenate([jnp.where(lane < HEAD_DIM, q, zero),
                          jnp.where(lane >= HEAD_DIM, q, zero)], axis=0)
    s = lax.dot_general(qs, k_ref[...], (((1,), (1,)), ((), ())), preferred_element_type=F32)
    m = jnp.max(s, axis=-1, keepdims=True)
    e = jnp.exp(s - m)
    inv = 1.0 / jnp.sum(e, axis=-1, keepdims=True)
    lv = lam_ref[...]
    lam = (jnp.exp(jnp.sum(lv[0:1] * lv[1:2], axis=-1, keepdims=True))
           - jnp.exp(jnp.sum(lv[2:3] * lv[3:4], axis=-1, keepdims=True)) + lam_init)
    w = e[:tq] * inv[:tq] - e[tq:] * (lam * inv[tq:])
    o = jnp.dot(w.astype(BF16), v_ref[...], preferred_element_type=F32)
    o = _rms(o, sg_ref[...]) * (1.0 - lam_init)
    o_ref[...] = o.astype(BF16)


def _attention(q, k, v, lam_vecs, subln_g, lam_init, j, dims, latent_out=None):
    b, s, c = dims
    tq = ATTN_TILE
    blocks_per_b = (s + c) // tq
    context_queries = latent_out is not None
    if context_queries:
        n_qt, n_keys = c // tq, c
        q_block = lambda bb, jq: bb * blocks_per_b + jq
        k_block = lambda bb: bb * ((s + c) // c)
    else:
        n_qt, n_keys = s // tq, s + c
        q_block = lambda bb, jq: bb * blocks_per_b + c // tq + jq
        k_block = lambda bb: bb
    kv_spec = pl.BlockSpec((None, n_keys, V_DIM), lambda bb, hd, jq: (hd, k_block(bb), 0))
    q_spec = pl.BlockSpec((None, tq, V_DIM), lambda bb, hd, jq: (hd, q_block(bb, jq), 0))
    in_specs = [q_spec, kv_spec, kv_spec, _stacked(lam_vecs, (j,)), _stacked(subln_g, (j,))]
    args = [q, k, v, lam_vecs, subln_g]
    aliases = {}
    if context_queries:
        in_specs.append(pl.BlockSpec(memory_space=pl.ANY))
        args.append(latent_out)
        aliases = {len(args) - 1: 0}
    return pl.pallas_call(
        functools.partial(_attn_kernel, lam_init=lam_init),
        grid=(b, N_HEADS, n_qt),
        in_specs=in_specs,
        out_specs=q_spec,
        out_shape=jax.ShapeDtypeStruct(q.shape, BF16),
        input_output_aliases=aliases,
        compiler_params=_params(3),
        name="ctx_attn" if context_queries else "lat_attn",
    )(*args)


def _oproj_kernel(o_ref, h_ref, gt_ref, w_ref, out_ref):
    o = jnp.concatenate([o_ref[hd] for hd in range(N_HEADS)], axis=-1)
    y = jnp.dot(o, w_ref[...], preferred_element_type=F32)
    out_ref[...] = h_ref[...] + gt_ref[...] * y


def _oproj(o_hm, h, mods, w_out, layer, j, dims, n_rows):
    d = h.shape[1]
    tm = ATTN_TILE
    row_of_tile, _, hm_block = _tile_maps(dims)
    return pl.pallas_call(
        _oproj_kernel,
        grid=(n_rows // tm,),
        in_specs=[
            pl.BlockSpec((N_HEADS, tm, V_DIM), lambda t: (0, hm_block(t), 0)),
            pl.BlockSpec((tm, d), lambda t: (t, 0)),
            _mod_spec(layer, 5, d, row_of_tile),
            _stacked(w_out, (j,), single_buffer=True),
        ],
        out_specs=pl.BlockSpec((tm, d), lambda t: (t, 0)),
        out_shape=jax.ShapeDtypeStruct((n_rows, d), F32),
        compiler_params=_params(1),
        name="attn_out",
    )(o_hm, h, mods, w_out)


def _sg_kernel(h_ref, sh_ref, sc_ref, gt_ref, g_ref, win_ref, lng_ref, lnb_ref, ws_ref, bs_ref, wout_ref,
               o_ref):
    tm = h_ref.shape[0]
    e = wout_ref.shape[0]
    gd = e // SG_GROUPS
    h = h_ref[...]
    xb = _modulated_bf16(h, g_ref[...], sh_ref[...], sc_ref[...])
    u = jax.nn.gelu(jnp.dot(xb, win_ref[:, :e], preferred_element_type=F32))
    v = jax.nn.gelu(jnp.dot(xb, win_ref[:, e:], preferred_element_type=F32))
    mu = jnp.mean(v, axis=-1, keepdims=True)
    vc = v - mu
    vn = vc * lax.rsqrt(jnp.mean(vc * vc, axis=-1, keepdims=True) + LN_EPS)
    vb = (vn * lng_ref[...] + lnb_ref[...]).astype(BF16)
    bias = bs_ref[...]
    rows = []
    for n in range(tm // CHUNK):
        r = slice(n * CHUNK, (n + 1) * CHUNK)
        cols = [jnp.dot(ws_ref[gi], vb[r, gi * gd:(gi + 1) * gd], preferred_element_type=F32)
                for gi in range(SG_GROUPS)]
        rows.append(jnp.concatenate(cols, axis=-1) + bias)
    mixed = jnp.concatenate(rows, axis=0)
    y = jnp.dot((u * mixed).astype(BF16), wout_ref[...], preferred_element_type=F32)
    o_ref[...] = h + gt_ref[...] * y


def _sg_mixer(h, mods, norm_g, w_in, ln_g, ln_b, w_s, bias_full, w_out, layer, j, n_rows, row_of_tile):
    d = h.shape[1]
    tm = ROW_TILE
    return pl.pallas_call(
        _sg_kernel,
        grid=(n_rows // tm,),
        in_specs=[
            pl.BlockSpec((tm, d), lambda t: (t, 0)),
            _mod_spec(layer, 3, d, row_of_tile),
            _mod_spec(layer, 4, d, row_of_tile),
            _mod_spec(layer, 5, d, row_of_tile),
            _stacked(norm_g, (layer, 1)),
            _stacked(w_in, (j,), single_buffer=True),
            _stacked(ln_g, (j,)),
            _stacked(ln_b, (j,)),
            _stacked(w_s, (j,)),
            _stacked(bias_full, (j,)),
            _stacked(w_out, (j,), single_buffer=True),
        ],
        out_specs=pl.BlockSpec((tm, d), lambda t: (t, 0)),
        out_shape=jax.ShapeDtypeStruct((n_rows, d), F32),
        compiler_params=_params(1),
        name="sg_mixer",
    )(h, mods, mods, mods, norm_g, w_in, ln_g, ln_b, w_s, bias_full, w_out)


def _rope_tables(s, tm):
    rows_n = s // GRID_W
    row = jnp.repeat(jnp.arange(rows_n), GRID_W)
    col = jnp.tile(jnp.arange(GRID_W), rows_n)
    n_freq = HEAD_DIM // 4
    inv = ROPE_THETA ** (-jnp.arange(n_freq, dtype=F32) / n_freq)
    pos = jnp.stack([row, col], axis=-1).astype(F32)
    ang = pos[:, :, None] * inv
    cos, sin = jnp.cos(ang), jnp.sin(ang)
    cos_l = jnp.broadcast_to(cos[:, None, :, None, :], (s, 2, 2, 2, n_freq)).reshape(s, V_DIM)
    sign = jnp.array([-1.0, 1.0], F32)[None, None, None, :, None]
    sin_l = jnp.broadcast_to(sin[:, None, :, None, :] * sign, (s, 2, 2, 2, n_freq)).reshape(s, V_DIM)
    cos_t = jnp.concatenate([cos_l, jnp.ones((tm, V_DIM), F32)], axis=0)
    sin_t = jnp.concatenate([sin_l, jnp.zeros((tm, V_DIM), F32)], axis=0)
    return cos_t, sin_t


def kernel(x, c, ctx, c_ctx, w_mod, b_mod, norm_g, w_ffn_gu, w_ffn_down, da_w_in, da_w_out, da_lambda,
           da_subln_g, sg_w_in, sg_ln_g, sg_ln_b, sg_w_s, sg_b_s, sg_w_out, final_g):
    b, s, d = x.shape
    cl = ctx.shape[1]
    depth = w_mod.shape[0]
    e = sg_w_out.shape[1]
    assert cl == ATTN_TILE and s % ROW_TILE == 0 and s % GRID_W == 0 and b < MOD_ROWS
    assert (b * cl) % ROW_TILE == 0 and ROW_TILE % CHUNK == 0 and cl % CHUNK == 0
    dims = (b, s, cl)
    n_lat, n_all = b * s, b * (s + cl)

    def row_of_tile(t):
        return jnp.minimum(t // (s // ROW_TILE), b)

    c_all = jnp.zeros((MOD_ROWS, d), F32).at[:b].set(c).at[b].set(c_ctx)
    mods = _mod_table(c_all, w_mod, b_mod).reshape(depth, MOD_ROWS, N_MOD, 1, d)
    cos_t, sin_t = _rope_tables(s, ATTN_TILE)

    norm_g4 = norm_g.reshape(depth, 3, 1, d)
    w_gu = w_ffn_gu.astype(BF16)
    w_down = w_ffn_down.astype(BF16)
    a_in, a_out = da_w_in.astype(BF16), da_w_out.astype(BF16)
    subln_g = da_subln_g.reshape(-1, 1, V_DIM)
    g_in, g_out, g_ws = sg_w_in.astype(BF16), sg_w_out.astype(BF16), sg_w_s.astype(BF16)
    ln_g, ln_b = sg_ln_g.reshape(-1, 1, e), sg_ln_b.reshape(-1, 1, e)
    bias_full = jnp.repeat(jnp.swapaxes(sg_b_s, 1, 2), e // SG_GROUPS, axis=2)

    last_ctx_layer = max(i for i in range(depth) if i % N_MIXERS == 0)
    h = jnp.concatenate([x.reshape(n_lat, d), ctx.reshape(b * cl, d)], axis=0)
    for i in range(depth):
        j = i // N_MIXERS
        rows_in = n_all if i <= last_ctx_layer else n_lat
        rows_out = n_all if i < last_ctx_layer else n_lat
        h = _ffn(h, mods, norm_g4, w_gu, w_down, i, 0, rows_in, row_of_tile)
        if i % N_MIXERS == 0:
            lam_init = 0.8 - 0.6 * math.exp(-0.3 * i)
            q, k, v = _qkv(h, mods, norm_g4, a_in, cos_t, sin_t, i, j, dims)
            o = _attention(q, k, v, da_lambda, subln_g, lam_init, j, dims)
            if i < last_ctx_layer:
                o = _attention(q, k, v, da_lambda, subln_g, lam_init, j, dims, latent_out=o)
            h = _oproj(o, h, mods, a_out, i, j, dims, rows_out)
        else:
            h = _sg_mixer(h, mods, norm_g4, g_in, ln_g, ln_b, g_ws, bias_full, g_out, i, j, rows_out,
                          row_of_tile)
        h = _ffn(h, mods, norm_g4, w_gu, w_down, i, 1, rows_out, row_of_tile,
                 final_g=final_g if i == depth - 1 else None)
    return h.reshape(b, s, d)
```

```python
import functools
import math

import jax
import jax.numpy as jnp
from jax import lax
from jax.experimental import pallas as pl
from jax.experimental.pallas import tpu as pltpu

F32 = jnp.float32
BF16 = jnp.bfloat16

GRID_W = 64
N_MIXERS = 2
N_HEADS = 8
HEAD_DIM = 64
V_DIM = 2 * HEAD_DIM
ROPE_THETA = 10000.0
CHUNK = 128
SG_GROUPS = 8
N_MOD = 9
RMS_EPS = 1e-6
LN_EPS = 1e-5

V7X_LANES = 128
V7X_VMEM_BYTES = 64 * 1024 * 1024
VMEM_LIMIT_BYTES = V7X_VMEM_BYTES - 8 * 1024 * 1024

MOD_ROWS = 16
ATTN_TILE = 256
ROW_TILE = 512
HEADS_PER_STEP = 4
LOG2_E = math.log2(math.e)


def _params(n_axes):
    return pltpu.CompilerParams(dimension_semantics=("arbitrary",) * n_axes,
                                vmem_limit_bytes=VMEM_LIMIT_BYTES)


def _rms(x, g):
    y = x * lax.rsqrt(jnp.mean(x * x, axis=-1, keepdims=True) + RMS_EPS)
    return y * g


def _modulated_bf16(h, g, shift, scale):
    return (_rms(h, g) * (1.0 + scale) + shift).astype(BF16)


def _stacked(arr, idx, single_buffer=False):
    tail = arr.shape[len(idx):]
    index = tuple(idx) + (0,) * len(tail)
    kwargs = dict(pipeline_mode=pl.Buffered(1)) if single_buffer else {}
    return pl.BlockSpec((None,) * len(idx) + tail, lambda *_: index, **kwargs)


def _mod_spec(layer, k, d, row_of_tile):
    return pl.BlockSpec((None, None, None, 1, d), lambda t: (layer, row_of_tile(t), k, 0, 0))


def _mod_kernel(c_ref, w_ref, b_ref, o_ref):
    sc = jax.nn.silu(c_ref[...]).astype(BF16)
    o_ref[...] = jnp.dot(sc, w_ref[...].astype(BF16), preferred_element_type=F32) + b_ref[...]


def _mod_table(c_all, w_mod, b_mod):
    depth, d, nd = w_mod.shape
    tn = d
    return pl.pallas_call(
        _mod_kernel,
        grid=(depth, nd // tn),
        in_specs=[
            pl.BlockSpec((MOD_ROWS, d), lambda i, n: (0, 0)),
            pl.BlockSpec((None, d, tn), lambda i, n: (i, 0, n)),
            pl.BlockSpec((None, 1, tn), lambda i, n: (i, 0, n)),
        ],
        out_specs=pl.BlockSpec((None, MOD_ROWS, tn), lambda i, n: (i, 0, n)),
        out_shape=jax.ShapeDtypeStruct((depth, MOD_ROWS, nd), F32),
        compiler_params=_params(2),
        name="mod_table",
    )(c_all, w_mod, b_mod.reshape(depth, 1, nd))


def _ffn_kernel(h_ref, sh_ref, sc_ref, gt_ref, g_ref, wgu_ref, wd_ref, *rest, n_chunks, final):
    if final:
        fg_ref, o_ref = rest
    else:
        (o_ref,) = rest
    d_ff = wd_ref.shape[0]
    h = h_ref[...]
    xb = _modulated_bf16(h, g_ref[...], sh_ref[...], sc_ref[...])
    fc = d_ff // n_chunks
    y = None
    for c in range(n_chunks):
        a = jnp.dot(xb, wgu_ref[:, c * fc:(c + 1) * fc], preferred_element_type=F32)
        u = jnp.dot(xb, wgu_ref[:, d_ff + c * fc:d_ff + (c + 1) * fc], preferred_element_type=F32)
        act = (jax.nn.silu(a) * u).astype(BF16)
        p = jnp.dot(act, wd_ref[c * fc:(c + 1) * fc, :], preferred_element_type=F32)
        y = p if y is None else y + p
    out = h + (0.5 * gt_ref[...]) * y
    if final:
        out = _rms(out, fg_ref[...])
    o_ref[...] = out


def _ffn(h, mods, norm_g, w_gu, w_down, layer, which, n_rows, row_of_tile, final_g=None):
    d = h.shape[1]
    tm = ROW_TILE
    k0 = 6 * which
    final = final_g is not None
    in_specs = [
        pl.BlockSpec((tm, d), lambda t: (t, 0)),
        _mod_spec(layer, k0, d, row_of_tile),
        _mod_spec(layer, k0 + 1, d, row_of_tile),
        _mod_spec(layer, k0 + 2, d, row_of_tile),
        _stacked(norm_g, (layer, 2 * which)),
        _stacked(w_gu, (layer, which), single_buffer=True),
        _stacked(w_down, (layer, which), single_buffer=True),
    ]
    args = [h, mods, mods, mods, norm_g, w_gu, w_down]
    if final:
        in_specs.append(pl.BlockSpec((1, d), lambda t: (0, 0)))
        args.append(final_g.reshape(1, d))
    return pl.pallas_call(
        functools.partial(_ffn_kernel, n_chunks=2, final=final),
        grid=(n_rows // tm,),
        in_specs=in_specs,
        out_specs=pl.BlockSpec((tm, d), lambda t: (t, 0)),
        out_shape=jax.ShapeDtypeStruct((n_rows, d), F32),
        compiler_params=_params(1),
        name="ffn",
    )(*args)


def _qkv_kernel(h_ref, sh_ref, sc_ref, g_ref, w_ref, cos_ref, sin_ref, q_ref, k_ref, v_ref):
    d = h_ref.shape[1]
    xb = _modulated_bf16(h_ref[...], g_ref[...], sh_ref[...], sc_ref[...])
    cos = cos_ref[...]
    sin = sin_ref[...]
    lane = lax.broadcasted_iota(jnp.int32, cos.shape, 1)
    first_half = (lane % (HEAD_DIM // 2)) < (HEAD_DIM // 4)

    def rope(x):
        partner = jnp.where(first_half,
                            pltpu.roll(x, V7X_LANES - HEAD_DIM // 4, axis=1),
                            pltpu.roll(x, HEAD_DIM // 4, axis=1))
        return x * cos + partner * sin

    q = jnp.dot(xb, w_ref[:, :d], preferred_element_type=F32)
    k = jnp.dot(xb, w_ref[:, d:2 * d], preferred_element_type=F32)
    v = jnp.dot(xb, w_ref[:, 2 * d:], preferred_element_type=F32)
    for hd in range(N_HEADS):
        sl = slice(hd * V_DIM, (hd + 1) * V_DIM)
        q_ref[hd] = (rope(q[:, sl]) * (LOG2_E * HEAD_DIM ** -0.5)).astype(BF16)
        k_ref[hd] = rope(k[:, sl]).astype(BF16)
        v_ref[hd] = v[:, sl].astype(BF16)


def _tile_maps(dims):
    b, s, c = dims
    tm = ATTN_TILE
    n_lat = b * s // tm
    per_b = s // tm
    blocks_per_b = (s + c) // tm

    def row_of_tile(t):
        return jnp.minimum(t // per_b, b)

    def rope_block(t):
        return jnp.where(t < n_lat, t % per_b, per_b)

    def hm_block(t):
        return jnp.where(t < n_lat, (t // per_b) * blocks_per_b + 1 + t % per_b, (t - n_lat) * blocks_per_b)

    return row_of_tile, rope_block, hm_block


def _qkv(h, mods, norm_g, w_in, cos_t, sin_t, layer, j, dims):
    b, s, c = dims
    d = h.shape[1]
    tm = ATTN_TILE
    row_of_tile, rope_block, hm_block = _tile_maps(dims)
    hm = jax.ShapeDtypeStruct((N_HEADS, b * (s + c), V_DIM), BF16)
    hm_spec = pl.BlockSpec((N_HEADS, tm, V_DIM), lambda t: (0, hm_block(t), 0))
    return pl.pallas_call(
        _qkv_kernel,
        grid=(b * (s + c) // tm,),
        in_specs=[
            pl.BlockSpec((tm, d), lambda t: (t, 0)),
            _mod_spec(layer, 3, d, row_of_tile),
            _mod_spec(layer, 4, d, row_of_tile),
            _stacked(norm_g, (layer, 1)),
            _stacked(w_in, (j,), single_buffer=True),
            pl.BlockSpec((tm, V_DIM), lambda t: (rope_block(t), 0)),
            pl.BlockSpec((tm, V_DIM), lambda t: (rope_block(t), 0)),
        ],
        out_specs=[hm_spec, hm_spec, hm_spec],
        out_shape=[hm, hm, hm],
        compiler_params=_params(1),
        name="qkv_rope",
    )(h, mods, mods, norm_g, w_in, cos_t, sin_t)


def _attn_kernel(q_ref, k_ref, v_ref, lam_ref, sg_ref, *rest, lam_init):
    o_ref = rest[-1]
    n_heads, tq, _ = q_ref.shape
    lv = lam_ref[...]
    lam = (jnp.exp(jnp.sum(lv[0:1] * lv[1:2], axis=-1, keepdims=True))
           - jnp.exp(jnp.sum(lv[2:3] * lv[3:4], axis=-1, keepdims=True)) + lam_init)
    lane = lax.broadcasted_iota(jnp.int32, (tq, V_DIM), 1)
    zero = jnp.zeros((tq, V_DIM), BF16)
    scores = []
    for hd in range(n_heads):
        q = q_ref[hd]
        qs = jnp.concatenate([jnp.where(lane < HEAD_DIM, q, zero),
                              jnp.where(lane >= HEAD_DIM, q, zero)], axis=0)
        scores.append(lax.dot_general(qs, k_ref[hd], (((1,), (1,)), ((), ())),
                                      preferred_element_type=F32))
    for hd in range(n_heads):
        s = scores[hd]
        e = jnp.exp2(s - jnp.max(s, axis=-1, keepdims=True))
        inv = 1.0 / jnp.sum(e, axis=-1, keepdims=True)
        w = e[:tq] - e[tq:] * (lam * inv[tq:] / inv[:tq])
        o = jnp.dot(w.astype(BF16), v_ref[hd], preferred_element_type=F32) * inv[:tq]
        o = _rms(o, sg_ref[...]) * (1.0 - lam_init)
        o_ref[hd] = o.astype(BF16)


def _attention(q, k, v, lam_vecs, subln_g, lam_init, j, dims, latent_out=None):
    b, s, c = dims
    tq = ATTN_TILE
    blocks_per_b = (s + c) // tq
    context_queries = latent_out is not None
    if context_queries:
        n_qt, n_keys = c // tq, c
        q_block = lambda bb, jq: bb * blocks_per_b + jq
        k_block = lambda bb: bb * ((s + c) // c)
    else:
        n_qt, n_keys = s // tq, s + c
        q_block = lambda bb, jq: bb * blocks_per_b + c // tq + jq
        k_block = lambda bb: bb
    hb = HEADS_PER_STEP
    kv_spec = pl.BlockSpec((hb, n_keys, V_DIM), lambda bb, hd, jq: (hd, k_block(bb), 0))
    q_spec = pl.BlockSpec((hb, tq, V_DIM), lambda bb, hd, jq: (hd, q_block(bb, jq), 0))
    in_specs = [q_spec, kv_spec, kv_spec, _stacked(lam_vecs, (j,)), _stacked(subln_g, (j,))]
    args = [q, k, v, lam_vecs, subln_g]
    aliases = {}
    if context_queries:
        in_specs.append(pl.BlockSpec(memory_space=pl.ANY))
        args.append(latent_out)
        aliases = {len(args) - 1: 0}
    return pl.pallas_call(
        functools.partial(_attn_kernel, lam_init=lam_init),
        grid=(b, N_HEADS // hb, n_qt),
        in_specs=in_specs,
        out_specs=q_spec,
        out_shape=jax.ShapeDtypeStruct(q.shape, BF16),
        input_output_aliases=aliases,
        compiler_params=_params(3),
        name="ctx_attn" if context_queries else "lat_attn",
    )(*args)


def _oproj_kernel(o_ref, h_ref, gt_ref, w_ref, out_ref):
    o = jnp.concatenate([o_ref[hd] for hd in range(N_HEADS)], axis=-1)
    y = jnp.dot(o, w_ref[...], preferred_element_type=F32)
    out_ref[...] = h_ref[...] + gt_ref[...] * y


def _oproj(o_hm, h, mods, w_out, layer, j, dims, n_rows):
    d = h.shape[1]
    tm = ATTN_TILE
    row_of_tile, _, hm_block = _tile_maps(dims)
    return pl.pallas_call(
        _oproj_kernel,
        grid=(n_rows // tm,),
        in_specs=[
            pl.BlockSpec((N_HEADS, tm, V_DIM), lambda t: (0, hm_block(t), 0)),
            pl.BlockSpec((tm, d), lambda t: (t, 0)),
            _mod_spec(layer, 5, d, row_of_tile),
            _stacked(w_out, (j,), single_buffer=True),
        ],
        out_specs=pl.BlockSpec((tm, d), lambda t: (t, 0)),
        out_shape=jax.ShapeDtypeStruct((n_rows, d), F32),
        compiler_params=_params(1),
        name="attn_out",
    )(o_hm, h, mods, w_out)


def _sg_kernel(h_ref, sh_ref, sc_ref, gt_ref, g_ref, win_ref, lng_ref, lnb_ref, ws_ref, bs_ref, wout_ref,
               o_ref):
    tm = h_ref.shape[0]
    e = wout_ref.shape[0]
    gd = e // SG_GROUPS
    h = h_ref[...]
    xb = _modulated_bf16(h, g_ref[...], sh_ref[...], sc_ref[...])
    u = jax.nn.gelu(jnp.dot(xb, win_ref[:, :e], preferred_element_type=F32))
    v = jax.nn.gelu(jnp.dot(xb, win_ref[:, e:], preferred_element_type=F32))
    mu = jnp.mean(v, axis=-1, keepdims=True)
    vc = v - mu
    vn = vc * lax.rsqrt(jnp.mean(vc * vc, axis=-1, keepdims=True) + LN_EPS)
    vb = (vn * lng_ref[...] + lnb_ref[...]).astype(BF16)
    bias = bs_ref[...]
    rows = []
    for n in range(tm // CHUNK):
        r = slice(n * CHUNK, (n + 1) * CHUNK)
        cols = [jnp.dot(ws_ref[gi], vb[r, gi * gd:(gi + 1) * gd], preferred_element_type=F32)
                for gi in range(SG_GROUPS)]
        rows.append(jnp.concatenate(cols, axis=-1) + bias)
    mixed = jnp.concatenate(rows, axis=0)
    y = jnp.dot((u * mixed).astype(BF16), wout_ref[...], preferred_element_type=F32)
    o_ref[...] = h + gt_ref[...] * y


def _sg_mixer(h, mods, norm_g, w_in, ln_g, ln_b, w_s, bias_full, w_out, layer, j, n_rows, row_of_tile):
    d = h.shape[1]
    tm = ROW_TILE
    return pl.pallas_call(
        _sg_kernel,
        grid=(n_rows // tm,),
        in_specs=[
            pl.BlockSpec((tm, d), lambda t: (t, 0)),
            _mod_spec(layer, 3, d, row_of_tile),
            _mod_spec(layer, 4, d, row_of_tile),
            _mod_spec(layer, 5, d, row_of_tile),
            _stacked(norm_g, (layer, 1)),
            _stacked(w_in, (j,), single_buffer=True),
            _stacked(ln_g, (j,)),
            _stacked(ln_b, (j,)),
            _stacked(w_s, (j,)),
            _stacked(bias_full, (j,)),
            _stacked(w_out, (j,), single_buffer=True),
        ],
        out_specs=pl.BlockSpec((tm, d), lambda t: (t, 0)),
        out_shape=jax.ShapeDtypeStruct((n_rows, d), F32),
        compiler_params=_params(1),
        name="sg_mixer",
    )(h, mods, mods, mods, norm_g, w_in, ln_g, ln_b, w_s, bias_full, w_out)


def _rope_tables(s, tm):
    rows_n = s // GRID_W
    row = jnp.repeat(jnp.arange(rows_n), GRID_W)
    col = jnp.tile(jnp.arange(GRID_W), rows_n)
    n_freq = HEAD_DIM // 4
    inv = ROPE_THETA ** (-jnp.arange(n_freq, dtype=F32) / n_freq)
    pos = jnp.stack([row, col], axis=-1).astype(F32)
    ang = pos[:, :, None] * inv
    cos, sin = jnp.cos(ang), jnp.sin(ang)
    cos_l = jnp.broadcast_to(cos[:, None, :, None, :], (s, 2, 2, 2, n_freq)).reshape(s, V_DIM)
    sign = jnp.array([-1.0, 1.0], F32)[None, None, None, :, None]
    sin_l = jnp.broadcast_to(sin[:, None, :, None, :] * sign, (s, 2, 2, 2, n_freq)).reshape(s, V_DIM)
    cos_t = jnp.concatenate([cos_l, jnp.ones((tm, V_DIM), F32)], axis=0)
    sin_t = jnp.concatenate([sin_l, jnp.zeros((tm, V_DIM), F32)], axis=0)
    return cos_t, sin_t


def kernel(x, c, ctx, c_ctx, w_mod, b_mod, norm_g, w_ffn_gu, w_ffn_down, da_w_in, da_w_out, da_lambda,
           da_subln_g, sg_w_in, sg_ln_g, sg_ln_b, sg_w_s, sg_b_s, sg_w_out, final_g):
    b, s, d = x.shape
    cl = ctx.shape[1]
    depth = w_mod.shape[0]
    e = sg_w_out.shape[1]
    assert cl == ATTN_TILE and s % ROW_TILE == 0 and s % GRID_W == 0 and b < MOD_ROWS
    assert (b * cl) % ROW_TILE == 0 and ROW_TILE % CHUNK == 0 and cl % CHUNK == 0
    dims = (b, s, cl)
    n_lat, n_all = b * s, b * (s + cl)

    def row_of_tile(t):
        return jnp.minimum(t // (s // ROW_TILE), b)

    c_all = jnp.zeros((MOD_ROWS, d), F32).at[:b].set(c).at[b].set(c_ctx)
    mods = _mod_table(c_all, w_mod, b_mod).reshape(depth, MOD_ROWS, N_MOD, 1, d)
    cos_t, sin_t = _rope_tables(s, ATTN_TILE)

    norm_g4 = norm_g.reshape(depth, 3, 1, d)
    w_gu = w_ffn_gu.astype(BF16)
    w_down = w_ffn_down.astype(BF16)
    a_in, a_out = da_w_in.astype(BF16), da_w_out.astype(BF16)
    subln_g = da_subln_g.reshape(-1, 1, V_DIM)
    g_in, g_out, g_ws = sg_w_in.astype(BF16), sg_w_out.astype(BF16), sg_w_s.astype(BF16)
    ln_g, ln_b = sg_ln_g.reshape(-1, 1, e), sg_ln_b.reshape(-1, 1, e)
    bias_full = jnp.repeat(jnp.swapaxes(sg_b_s, 1, 2), e // SG_GROUPS, axis=2)

    last_ctx_layer = max(i for i in range(depth) if i % N_MIXERS == 0)
    h = jnp.concatenate([x.reshape(n_lat, d), ctx.reshape(b * cl, d)], axis=0)
    for i in range(depth):
        j = i // N_MIXERS
        rows_in = n_all if i <= last_ctx_layer else n_lat
        rows_out = n_all if i < last_ctx_layer else n_lat
        h = _ffn(h, mods, norm_g4, w_gu, w_down, i, 0, rows_in, row_of_tile)
        if i % N_MIXERS == 0:
            lam_init = 0.8 - 0.6 * math.exp(-0.3 * i)
            q, k, v = _qkv(h, mods, norm_g4, a_in, cos_t, sin_t, i, j, dims)
            o = _attention(q, k, v, da_lambda, subln_g, lam_init, j, dims)
            if i < last_ctx_layer:
                o = _attention(q, k, v, da_lambda, subln_g, lam_init, j, dims, latent_out=o)
            h = _oproj(o, h, mods, a_out, i, j, dims, rows_out)
        else:
            h = _sg_mixer(h, mods, norm_g4, g_in, ln_g, ln_b, g_ws, bias_full, g_out, i, j, rows_out,
                          row_of_tile)
        h = _ffn(h, mods, norm_g4, w_gu, w_down, i, 1, rows_out, row_of_tile,
                 final_g=final_g if i == depth - 1 else None)
    return h.reshape(b, s, d)
```

```python
import functools
import math

import jax
import jax.numpy as jnp
from jax import lax
from jax.experimental import pallas as pl
from jax.experimental.pallas import tpu as pltpu

F32 = jnp.float32
BF16 = jnp.bfloat16

GRID_W = 64
N_MIXERS = 2
N_HEADS = 8
HEAD_DIM = 64
V_DIM = 2 * HEAD_DIM
ROPE_THETA = 10000.0
CHUNK = 128
SG_GROUPS = 8
N_MOD = 9
RMS_EPS = 1e-6
LN_EPS = 1e-5

V7X_LANES = 128
V7X_VMEM_BYTES = 64 * 1024 * 1024
VMEM_LIMIT_BYTES = V7X_VMEM_BYTES - 8 * 1024 * 1024

MOD_ROWS = 16
ATTN_TILE = 256
ROW_TILE = 512
HEADS_PER_STEP = 4
LOG2_E = math.log2(math.e)


def _params(n_axes):
    return pltpu.CompilerParams(dimension_semantics=("arbitrary",) * n_axes,
                                vmem_limit_bytes=VMEM_LIMIT_BYTES)


def _rms(x, g):
    y = x * lax.rsqrt(jnp.mean(x * x, axis=-1, keepdims=True) + RMS_EPS)
    return y * g


def _modulated_bf16(h, g, shift, scale):
    return (_rms(h, g) * (1.0 + scale) + shift).astype(BF16)


def _stacked(arr, idx, single_buffer=False):
    tail = arr.shape[len(idx):]
    index = tuple(idx) + (0,) * len(tail)
    kwargs = dict(pipeline_mode=pl.Buffered(1)) if single_buffer else {}
    return pl.BlockSpec((None,) * len(idx) + tail, lambda *_: index, **kwargs)


def _mod_spec(layer, k, d, row_of_tile):
    return pl.BlockSpec((None, None, None, 1, d), lambda t: (layer, row_of_tile(t), k, 0, 0))


def _mod_kernel(c_ref, w_ref, b_ref, o_ref):
    sc = jax.nn.silu(c_ref[...]).astype(BF16)
    o_ref[...] = jnp.dot(sc, w_ref[...].astype(BF16), preferred_element_type=F32) + b_ref[...]


def _mod_table(c_all, w_mod, b_mod):
    depth, d, nd = w_mod.shape
    tn = d
    return pl.pallas_call(
        _mod_kernel,
        grid=(depth, nd // tn),
        in_specs=[
            pl.BlockSpec((MOD_ROWS, d), lambda i, n: (0, 0)),
            pl.BlockSpec((None, d, tn), lambda i, n: (i, 0, n)),
            pl.BlockSpec((None, 1, tn), lambda i, n: (i, 0, n)),
        ],
        out_specs=pl.BlockSpec((None, MOD_ROWS, tn), lambda i, n: (i, 0, n)),
        out_shape=jax.ShapeDtypeStruct((depth, MOD_ROWS, nd), F32),
        compiler_params=_params(2),
        name="mod_table",
    )(c_all, w_mod, b_mod.reshape(depth, 1, nd))


def _ffn_kernel(*refs, n_first, final):
    refs = list(refs)
    o_ref = refs.pop()
    fg_ref = refs.pop() if final else None
    h_ref = refs.pop(0)
    h = h_ref[...]
    if n_first is not None:
        h = jnp.where(pl.program_id(0) < n_first, h, refs.pop(0)[...])
    sh_ref, sc_ref, gt_ref, g_ref, wgu_ref, wd_ref = refs
    d_ff = wd_ref.shape[0]
    xb = _modulated_bf16(h, g_ref[...], sh_ref[...], sc_ref[...])
    a = jnp.dot(xb, wgu_ref[:, :d_ff], preferred_element_type=F32)
    u = jnp.dot(xb, wgu_ref[:, d_ff:], preferred_element_type=F32)
    act = (jax.nn.silu(a) * u).astype(BF16)
    y = jnp.dot(act, wd_ref[...], preferred_element_type=F32)
    out = h + (0.5 * gt_ref[...]) * y
    if final:
        out = _rms(out, fg_ref[...])
    o_ref[...] = out


def _ffn(h, mods, norm_g, w_gu, w_down, layer, which, n_rows, row_of_tile, final_g=None, h_tail=None):
    d = h.shape[1]
    tm = ROW_TILE
    k0 = 6 * which
    final = final_g is not None
    n_first = None
    h_specs, h_args = [pl.BlockSpec((tm, d), lambda t: (t, 0))], [h]
    if h_tail is not None:
        n_first = h.shape[0] // tm
        h_specs = [pl.BlockSpec((tm, d), lambda t: (jnp.minimum(t, n_first - 1), 0)),
                   pl.BlockSpec((tm, d), lambda t: (jnp.maximum(t - n_first, 0), 0))]
        h_args = [h, h_tail]
    in_specs = h_specs + [
        _mod_spec(layer, k0, d, row_of_tile),
        _mod_spec(layer, k0 + 1, d, row_of_tile),
        _mod_spec(layer, k0 + 2, d, row_of_tile),
        _stacked(norm_g, (layer, 2 * which)),
        _stacked(w_gu, (layer, which), single_buffer=True),
        _stacked(w_down, (layer, which), single_buffer=True),
    ]
    args = h_args + [mods, mods, mods, norm_g, w_gu, w_down]
    if final:
        in_specs.append(pl.BlockSpec((1, d), lambda t: (0, 0)))
        args.append(final_g.reshape(1, d))
    return pl.pallas_call(
        functools.partial(_ffn_kernel, n_first=n_first, final=final),
        grid=(n_rows // tm,),
        in_specs=in_specs,
        out_specs=pl.BlockSpec((tm, d), lambda t: (t, 0)),
        out_shape=jax.ShapeDtypeStruct((n_rows, d), F32),
        compiler_params=_params(1),
        name="ffn",
    )(*args)


def _qkv_kernel(h_ref, sh_ref, sc_ref, g_ref, w_ref, cos_ref, sin_ref, q_ref, k_ref, v_ref):
    d = h_ref.shape[1]
    xb = _modulated_bf16(h_ref[...], g_ref[...], sh_ref[...], sc_ref[...])
    cos = cos_ref[...]
    sin = sin_ref[...]
    lane = lax.broadcasted_iota(jnp.int32, cos.shape, 1)
    first_half = (lane % (HEAD_DIM // 2)) < (HEAD_DIM // 4)

    def rope(x):
        partner = jnp.where(first_half,
                            pltpu.roll(x, V7X_LANES - HEAD_DIM // 4, axis=1),
                            pltpu.roll(x, HEAD_DIM // 4, axis=1))
        return x * cos + partner * sin

    q = jnp.dot(xb, w_ref[:, :d], preferred_element_type=F32)
    k = jnp.dot(xb, w_ref[:, d:2 * d], preferred_element_type=F32)
    v = jnp.dot(xb, w_ref[:, 2 * d:], preferred_element_type=F32)
    for hd in range(N_HEADS):
        sl = slice(hd * V_DIM, (hd + 1) * V_DIM)
        q_ref[hd] = (rope(q[:, sl]) * (LOG2_E * HEAD_DIM ** -0.5)).astype(BF16)
        k_ref[hd] = rope(k[:, sl]).astype(BF16)
        v_ref[hd] = v[:, sl].astype(BF16)


def _tile_maps(dims):
    b, s, c = dims
    tm = ATTN_TILE
    n_lat = b * s // tm
    per_b = s // tm
    blocks_per_b = (s + c) // tm

    def row_of_tile(t):
        return jnp.minimum(t // per_b, b)

    def rope_block(t):
        return jnp.where(t < n_lat, t % per_b, per_b)

    def hm_block(t):
        return jnp.where(t < n_lat, (t // per_b) * blocks_per_b + 1 + t % per_b, (t - n_lat) * blocks_per_b)

    return row_of_tile, rope_block, hm_block


def _qkv(h, mods, norm_g, w_in, cos_t, sin_t, layer, j, dims):
    b, s, c = dims
    d = h.shape[1]
    tm = ATTN_TILE
    row_of_tile, rope_block, hm_block = _tile_maps(dims)
    hm = jax.ShapeDtypeStruct((N_HEADS, b * (s + c), V_DIM), BF16)
    hm_spec = pl.BlockSpec((N_HEADS, tm, V_DIM), lambda t: (0, hm_block(t), 0))
    return pl.pallas_call(
        _qkv_kernel,
        grid=(b * (s + c) // tm,),
        in_specs=[
            pl.BlockSpec((tm, d), lambda t: (t, 0)),
            _mod_spec(layer, 3, d, row_of_tile),
            _mod_spec(layer, 4, d, row_of_tile),
            _stacked(norm_g, (layer, 1)),
            _stacked(w_in, (j,), single_buffer=True),
            pl.BlockSpec((tm, V_DIM), lambda t: (rope_block(t), 0)),
            pl.BlockSpec((tm, V_DIM), lambda t: (rope_block(t), 0)),
        ],
        out_specs=[hm_spec, hm_spec, hm_spec],
        out_shape=[hm, hm, hm],
        compiler_params=_params(1),
        name="qkv_rope",
    )(h, mods, mods, norm_g, w_in, cos_t, sin_t)


def _attn_kernel(q_ref, k_ref, v_ref, lam_ref, sg_ref, o_ref, *, lam_init, ctx_len, ctx_tile):
    n_heads, tq, _ = q_ref.shape
    lv = lam_ref[...]
    lam = (jnp.exp(jnp.sum(lv[0:1] * lv[1:2], axis=-1, keepdims=True))
           - jnp.exp(jnp.sum(lv[2:3] * lv[3:4], axis=-1, keepdims=True)) + lam_init)
    lane = lax.broadcasted_iota(jnp.int32, (tq, V_DIM), 1)
    zero = jnp.zeros((tq, V_DIM), BF16)

    def attend(n_keys):
        scores = []
        for hd in range(n_heads):
            q = q_ref[hd]
            qs = jnp.concatenate([jnp.where(lane < HEAD_DIM, q, zero),
                                  jnp.where(lane >= HEAD_DIM, q, zero)], axis=0)
            scores.append(lax.dot_general(qs, k_ref[hd, :n_keys, :], (((1,), (1,)), ((), ())),
                                          preferred_element_type=F32))
        for hd in range(n_heads):
            s = scores[hd]
            e = jnp.exp2(s - jnp.max(s, axis=-1, keepdims=True))
            inv = 1.0 / jnp.sum(e, axis=-1, keepdims=True)
            w = e[:tq] - e[tq:] * (lam * inv[tq:] / inv[:tq])
            o = jnp.dot(w.astype(BF16), v_ref[hd, :n_keys, :], preferred_element_type=F32) * inv[:tq]
            o = _rms(o, sg_ref[...]) * (1.0 - lam_init)
            o_ref[hd] = o.astype(BF16)

    if ctx_tile:
        is_ctx = pl.program_id(2) == 0
        pl.when(is_ctx)(lambda: attend(ctx_len))
        pl.when(jnp.logical_not(is_ctx))(lambda: attend(k_ref.shape[1]))
    else:
        attend(k_ref.shape[1])


def _attention(q, k, v, lam_vecs, subln_g, lam_init, j, dims, context_queries):
    b, s, c = dims
    tq = ATTN_TILE
    blocks_per_b = (s + c) // tq
    lat_per_b = s // tq
    if context_queries:
        n_qt, out_rows = blocks_per_b, b * (s + c)
        q_block = lambda bb, jq: bb * blocks_per_b + jq
        o_block = q_block
    else:
        n_qt, out_rows = lat_per_b, b * s
        q_block = lambda bb, jq: bb * blocks_per_b + c // tq + jq
        o_block = lambda bb, jq: bb * lat_per_b + jq
    hb = HEADS_PER_STEP
    kv_spec = pl.BlockSpec((hb, s + c, V_DIM), lambda bb, hg, jq: (hg, bb, 0))
    return pl.pallas_call(
        functools.partial(_attn_kernel, lam_init=lam_init, ctx_len=c, ctx_tile=context_queries),
        grid=(b, N_HEADS // hb, n_qt),
        in_specs=[pl.BlockSpec((hb, tq, V_DIM), lambda bb, hg, jq: (hg, q_block(bb, jq), 0)),
                  kv_spec, kv_spec, _stacked(lam_vecs, (j,)), _stacked(subln_g, (j,))],
        out_specs=pl.BlockSpec((hb, tq, V_DIM), lambda bb, hg, jq: (hg, o_block(bb, jq), 0)),
        out_shape=jax.ShapeDtypeStruct((N_HEADS, out_rows, V_DIM), BF16),
        compiler_params=_params(3),
        name="attention",
    )(q, k, v, lam_vecs, subln_g)


def _oproj_kernel(o_ref, h_ref, gt_ref, w_ref, out_ref):
    o = jnp.concatenate([o_ref[hd] for hd in range(N_HEADS)], axis=-1)
    y = jnp.dot(o, w_ref[...], preferred_element_type=F32)
    out_ref[...] = h_ref[...] + gt_ref[...] * y


def _oproj(o_hm, h, mods, w_out, layer, j, dims, n_rows, latent_only):
    d = h.shape[1]
    tm = ATTN_TILE
    row_of_tile, _, hm_block = _tile_maps(dims)
    o_block = (lambda t: t) if latent_only else hm_block
    return pl.pallas_call(
        _oproj_kernel,
        grid=(n_rows // tm,),
        in_specs=[
            pl.BlockSpec((N_HEADS, tm, V_DIM), lambda t: (0, o_block(t), 0)),
            pl.BlockSpec((tm, d), lambda t: (t, 0)),
            _mod_spec(layer, 5, d, row_of_tile),
            _stacked(w_out, (j,), single_buffer=True),
        ],
        out_specs=pl.BlockSpec((tm, d), lambda t: (t, 0)),
        out_shape=jax.ShapeDtypeStruct((n_rows, d), F32),
        compiler_params=_params(1),
        name="attn_out",
    )(o_hm, h, mods, w_out)


def _sg_kernel(h_ref, sh_ref, sc_ref, gt_ref, g_ref, win_ref, lng_ref, lnb_ref, ws_ref, bs_ref, wout_ref,
               o_ref):
    tm = h_ref.shape[0]
    e = wout_ref.shape[0]
    gd = e // SG_GROUPS
    h = h_ref[...]
    xb = _modulated_bf16(h, g_ref[...], sh_ref[...], sc_ref[...])
    u = jax.nn.gelu(jnp.dot(xb, win_ref[:, :e], preferred_element_type=F32))
    v = jax.nn.gelu(jnp.dot(xb, win_ref[:, e:], preferred_element_type=F32))
    mu = jnp.mean(v, axis=-1, keepdims=True)
    vc = v - mu
    vn = vc * lax.rsqrt(jnp.mean(vc * vc, axis=-1, keepdims=True) + LN_EPS)
    vb = (vn * lng_ref[...] + lnb_ref[...]).astype(BF16)
    bias = bs_ref[...]
    rows = []
    for n in range(tm // CHUNK):
        r = slice(n * CHUNK, (n + 1) * CHUNK)
        cols = [jnp.dot(ws_ref[gi], vb[r, gi * gd:(gi + 1) * gd], preferred_element_type=F32)
                for gi in range(SG_GROUPS)]
        rows.append(jnp.concatenate(cols, axis=-1) + bias)
    mixed = jnp.concatenate(rows, axis=0)
    y = jnp.dot((u * mixed).astype(BF16), wout_ref[...], preferred_element_type=F32)
    o_ref[...] = h + gt_ref[...] * y


def _sg_mixer(h, mods, norm_g, w_in, ln_g, ln_b, w_s, bias_full, w_out, layer, j, n_rows, row_of_tile):
    d = h.shape[1]
    tm = ROW_TILE
    return pl.pallas_call(
        _sg_kernel,
        grid=(n_rows // tm,),
        in_specs=[
            pl.BlockSpec((tm, d), lambda t: (t, 0)),
            _mod_spec(layer, 3, d, row_of_tile),
            _mod_spec(layer, 4, d, row_of_tile),
            _mod_spec(layer, 5, d, row_of_tile),
            _stacked(norm_g, (layer, 1)),
            _stacked(w_in, (j,), single_buffer=True),
            _stacked(ln_g, (j,)),
            _stacked(ln_b, (j,)),
            _stacked(w_s, (j,)),
            _stacked(bias_full, (j,)),
            _stacked(w_out, (j,), single_buffer=True),
        ],
        out_specs=pl.BlockSpec((tm, d), lambda t: (t, 0)),
        out_shape=jax.ShapeDtypeStruct((n_rows, d), F32),
        compiler_params=_params(1),
        name="sg_mixer",
    )(h, mods, mods, mods, norm_g, w_in, ln_g, ln_b, w_s, bias_full, w_out)


def _rope_tables(s, tm):
    rows_n = s // GRID_W
    row = jnp.repeat(jnp.arange(rows_n), GRID_W)
    col = jnp.tile(jnp.arange(GRID_W), rows_n)
    n_freq = HEAD_DIM // 4
    inv = ROPE_THETA ** (-jnp.arange(n_freq, dtype=F32) / n_freq)
    pos = jnp.stack([row, col], axis=-1).astype(F32)
    ang = pos[:, :, None] * inv
    cos, sin = jnp.cos(ang), jnp.sin(ang)
    cos_l = jnp.broadcast_to(cos[:, None, :, None, :], (s, 2, 2, 2, n_freq)).reshape(s, V_DIM)
    sign = jnp.array([-1.0, 1.0], F32)[None, None, None, :, None]
    sin_l = jnp.broadcast_to(sin[:, None, :, None, :] * sign, (s, 2, 2, 2, n_freq)).reshape(s, V_DIM)
    cos_t = jnp.concatenate([cos_l, jnp.ones((tm, V_DIM), F32)], axis=0)
    sin_t = jnp.concatenate([sin_l, jnp.zeros((tm, V_DIM), F32)], axis=0)
    return cos_t, sin_t


def kernel(x, c, ctx, c_ctx, w_mod, b_mod, norm_g, w_ffn_gu, w_ffn_down, da_w_in, da_w_out, da_lambda,
           da_subln_g, sg_w_in, sg_ln_g, sg_ln_b, sg_w_s, sg_b_s, sg_w_out, final_g):
    b, s, d = x.shape
    cl = ctx.shape[1]
    depth = w_mod.shape[0]
    e = sg_w_out.shape[1]
    assert cl == ATTN_TILE and s % ROW_TILE == 0 and s % GRID_W == 0 and b < MOD_ROWS
    assert (b * cl) % ROW_TILE == 0 and ROW_TILE % CHUNK == 0 and cl % CHUNK == 0
    dims = (b, s, cl)
    n_lat, n_all = b * s, b * (s + cl)

    def row_of_tile(t):
        return jnp.minimum(t // (s // ROW_TILE), b)

    c_all = jnp.zeros((MOD_ROWS, d), F32).at[:b].set(c).at[b].set(c_ctx)
    mods = _mod_table(c_all, w_mod, b_mod).reshape(depth, MOD_ROWS, N_MOD, 1, d)
    cos_t, sin_t = _rope_tables(s, ATTN_TILE)

    norm_g4 = norm_g.reshape(depth, 3, 1, d)
    w_gu = w_ffn_gu.astype(BF16)
    w_down = w_ffn_down.astype(BF16)
    a_in, a_out = da_w_in.astype(BF16), da_w_out.astype(BF16)
    subln_g = da_subln_g.reshape(-1, 1, V_DIM)
    g_in, g_out, g_ws = sg_w_in.astype(BF16), sg_w_out.astype(BF16), sg_w_s.astype(BF16)
    ln_g, ln_b = sg_ln_g.reshape(-1, 1, e), sg_ln_b.reshape(-1, 1, e)
    bias_full = jnp.repeat(jnp.swapaxes(sg_b_s, 1, 2), e // SG_GROUPS, axis=2)

    last_ctx_layer = max(i for i in range(depth) if i % N_MIXERS == 0)
    h = x.reshape(n_lat, d)
    for i in range(depth):
        j = i // N_MIXERS
        rows_in = n_all if i <= last_ctx_layer else n_lat
        rows_out = n_all if i < last_ctx_layer else n_lat
        h_tail = ctx.reshape(b * cl, d) if i == 0 and rows_in == n_all else None
        h = _ffn(h, mods, norm_g4, w_gu, w_down, i, 0, rows_in, row_of_tile, h_tail=h_tail)
        if i % N_MIXERS == 0:
            lam_init = 0.8 - 0.6 * math.exp(-0.3 * i)
            q, k, v = _qkv(h, mods, norm_g4, a_in, cos_t, sin_t, i, j, dims)
            full = i < last_ctx_layer
            o = _attention(q, k, v, da_lambda, subln_g, lam_init, j, dims, context_queries=full)
            h = _oproj(o, h, mods, a_out, i, j, dims, rows_out, latent_only=not full)
        else:
            h = _sg_mixer(h, mods, norm_g4, g_in, ln_g, ln_b, g_ws, bias_full, g_out, i, j, rows_out,
                          row_of_tile)
        h = _ffn(h, mods, norm_g4, w_gu, w_down, i, 1, rows_out, row_of_tile,
                 final_g=final_g if i == depth - 1 else None)
    return h.reshape(b, s, d)
```

```python
import functools
import math

import jax
import jax.numpy as jnp
from jax import lax
from jax.experimental import pallas as pl
from jax.experimental.pallas import tpu as pltpu

F32 = jnp.float32
BF16 = jnp.bfloat16

GRID_W = 64
N_MIXERS = 2
N_HEADS = 8
HEAD_DIM = 64
V_DIM = 2 * HEAD_DIM
ROPE_THETA = 10000.0
CHUNK = 128
SG_GROUPS = 8
N_MOD = 9
RMS_EPS = 1e-6
LN_EPS = 1e-5

V7X_LANES = 128
V7X_VMEM_BYTES = 64 * 1024 * 1024
VMEM_LIMIT_BYTES = V7X_VMEM_BYTES - 8 * 1024 * 1024

MOD_ROWS = 16
ATTN_TILE = 256
ROW_TILE = 512
HEADS_PER_STEP = 4
LOG2_E = math.log2(math.e)
MAX_UNSHIFTED_SCORE = 40.0
NORM_MARGIN = 1.05


def _params(n_axes):
    return pltpu.CompilerParams(dimension_semantics=("arbitrary",) * n_axes,
                                vmem_limit_bytes=VMEM_LIMIT_BYTES)


def _rms(x, g):
    y = x * lax.rsqrt(jnp.mean(x * x, axis=-1, keepdims=True) + RMS_EPS)
    return y * g


def _modulated_bf16(h, g, shift, scale):
    return (_rms(h, g) * (1.0 + scale) + shift).astype(BF16)


def _stacked(arr, idx, single_buffer=False):
    tail = arr.shape[len(idx):]
    index = tuple(idx) + (0,) * len(tail)
    kwargs = dict(pipeline_mode=pl.Buffered(1)) if single_buffer else {}
    return pl.BlockSpec((None,) * len(idx) + tail, lambda *_: index, **kwargs)


def _mod_spec(layer, k, d, row_of_tile):
    return pl.BlockSpec((None, None, None, 1, d), lambda t: (layer, row_of_tile(t), k, 0, 0))


def _mod_kernel(c_ref, w_ref, b_ref, o_ref):
    sc = jax.nn.silu(c_ref[...]).astype(BF16)
    o_ref[...] = jnp.dot(sc, w_ref[...].astype(BF16), preferred_element_type=F32) + b_ref[...]


def _mod_table(c_all, w_mod, b_mod):
    depth, d, nd = w_mod.shape
    tn = d
    return pl.pallas_call(
        _mod_kernel,
        grid=(depth, nd // tn),
        in_specs=[
            pl.BlockSpec((MOD_ROWS, d), lambda i, n: (0, 0)),
            pl.BlockSpec((None, d, tn), lambda i, n: (i, 0, n)),
            pl.BlockSpec((None, 1, tn), lambda i, n: (i, 0, n)),
        ],
        out_specs=pl.BlockSpec((None, MOD_ROWS, tn), lambda i, n: (i, 0, n)),
        out_shape=jax.ShapeDtypeStruct((depth, MOD_ROWS, nd), F32),
        compiler_params=_params(2),
        name="mod_table",
    )(c_all, w_mod, b_mod.reshape(depth, 1, nd))


def _ffn_kernel(*refs, n_first, final):
    refs = list(refs)
    o_ref = refs.pop()
    fg_ref = refs.pop() if final else None
    h_ref = refs.pop(0)
    h = h_ref[...]
    if n_first is not None:
        h = jnp.where(pl.program_id(0) < n_first, h, refs.pop(0)[...])
    sh_ref, sc_ref, gt_ref, g_ref, wgu_ref, wd_ref = refs
    d_ff = wd_ref.shape[0]
    xb = _modulated_bf16(h, g_ref[...], sh_ref[...], sc_ref[...])
    a = jnp.dot(xb, wgu_ref[:, :d_ff], preferred_element_type=F32)
    u = jnp.dot(xb, wgu_ref[:, d_ff:], preferred_element_type=F32)
    act = (jax.nn.silu(a) * u).astype(BF16)
    y = jnp.dot(act, wd_ref[...], preferred_element_type=F32)
    out = h + (0.5 * gt_ref[...]) * y
    if final:
        out = _rms(out, fg_ref[...])
    o_ref[...] = out


def _ffn(h, mods, norm_g, w_gu, w_down, layer, which, n_rows, row_of_tile, final_g=None, h_tail=None):
    d = h.shape[1]
    tm = ROW_TILE
    k0 = 6 * which
    final = final_g is not None
    n_first = None
    h_specs, h_args = [pl.BlockSpec((tm, d), lambda t: (t, 0))], [h]
    if h_tail is not None:
        n_first = h.shape[0] // tm
        h_specs = [pl.BlockSpec((tm, d), lambda t: (jnp.minimum(t, n_first - 1), 0)),
                   pl.BlockSpec((tm, d), lambda t: (jnp.maximum(t - n_first, 0), 0))]
        h_args = [h, h_tail]
    in_specs = h_specs + [
        _mod_spec(layer, k0, d, row_of_tile),
        _mod_spec(layer, k0 + 1, d, row_of_tile),
        _mod_spec(layer, k0 + 2, d, row_of_tile),
        _stacked(norm_g, (layer, 2 * which)),
        _stacked(w_gu, (layer, which), single_buffer=True),
        _stacked(w_down, (layer, which), single_buffer=True),
    ]
    args = h_args + [mods, mods, mods, norm_g, w_gu, w_down]
    if final:
        in_specs.append(pl.BlockSpec((1, d), lambda t: (0, 0)))
        args.append(final_g.reshape(1, d))
    return pl.pallas_call(
        functools.partial(_ffn_kernel, n_first=n_first, final=final),
        grid=(n_rows // tm,),
        in_specs=in_specs,
        out_specs=pl.BlockSpec((tm, d), lambda t: (t, 0)),
        out_shape=jax.ShapeDtypeStruct((n_rows, d), F32),
        compiler_params=_params(1),
        name="ffn",
    )(*args)


def _qkv_kernel(h_ref, sh_ref, sc_ref, g_ref, w_ref, cos_ref, sin_ref, q_ref, k_ref, v_ref):
    d = h_ref.shape[1]
    xb = _modulated_bf16(h_ref[...], g_ref[...], sh_ref[...], sc_ref[...])
    cos = cos_ref[...]
    sin = sin_ref[...]
    lane = lax.broadcasted_iota(jnp.int32, cos.shape, 1)
    first_half = (lane % (HEAD_DIM // 2)) < (HEAD_DIM // 4)

    def rope(x):
        partner = jnp.where(first_half,
                            pltpu.roll(x, V7X_LANES - HEAD_DIM // 4, axis=1),
                            pltpu.roll(x, HEAD_DIM // 4, axis=1))
        return x * cos + partner * sin

    q = jnp.dot(xb, w_ref[:, :d], preferred_element_type=F32)
    k = jnp.dot(xb, w_ref[:, d:2 * d], preferred_element_type=F32)
    v = jnp.dot(xb, w_ref[:, 2 * d:], preferred_element_type=F32)
    for hd in range(N_HEADS):
        sl = slice(hd * V_DIM, (hd + 1) * V_DIM)
        q_ref[hd] = (rope(q[:, sl]) * (LOG2_E * HEAD_DIM ** -0.5)).astype(BF16)
        k_ref[hd] = rope(k[:, sl]).astype(BF16)
        v_ref[hd] = v[:, sl].astype(BF16)


def _tile_maps(dims):
    b, s, c = dims
    tm = ATTN_TILE
    n_lat = b * s // tm
    per_b = s // tm
    blocks_per_b = (s + c) // tm

    def row_of_tile(t):
        return jnp.minimum(t // per_b, b)

    def rope_block(t):
        return jnp.where(t < n_lat, t % per_b, per_b)

    def hm_block(t):
        return jnp.where(t < n_lat, (t // per_b) * blocks_per_b + 1 + t % per_b, (t - n_lat) * blocks_per_b)

    return row_of_tile, rope_block, hm_block


def _qkv(h, mods, norm_g, w_in, cos_t, sin_t, layer, j, dims):
    b, s, c = dims
    d = h.shape[1]
    tm = ATTN_TILE
    row_of_tile, rope_block, hm_block = _tile_maps(dims)
    hm = jax.ShapeDtypeStruct((N_HEADS, b * (s + c), V_DIM), BF16)
    hm_spec = pl.BlockSpec((N_HEADS, tm, V_DIM), lambda t: (0, hm_block(t), 0))
    return pl.pallas_call(
        _qkv_kernel,
        grid=(b * (s + c) // tm,),
        in_specs=[
            pl.BlockSpec((tm, d), lambda t: (t, 0)),
            _mod_spec(layer, 3, d, row_of_tile),
            _mod_spec(layer, 4, d, row_of_tile),
            _stacked(norm_g, (layer, 1)),
            _stacked(w_in, (j,), single_buffer=True),
            pl.BlockSpec((tm, V_DIM), lambda t: (rope_block(t), 0)),
            pl.BlockSpec((tm, V_DIM), lambda t: (rope_block(t), 0)),
        ],
        out_specs=[hm_spec, hm_spec, hm_spec],
        out_shape=[hm, hm, hm],
        compiler_params=_params(1),
        name="qkv_rope",
    )(h, mods, mods, norm_g, w_in, cos_t, sin_t)


def _attn_kernel(q_ref, qall_ref, k_ref, v_ref, lam_ref, sg_ref, o_ref, bounded_scr, *, lam_init, ctx_len,
                 ctx_tile):
    n_heads, tq, _ = q_ref.shape
    lv = lam_ref[...]
    lam = (jnp.exp(jnp.sum(lv[0:1] * lv[1:2], axis=-1, keepdims=True))
           - jnp.exp(jnp.sum(lv[2:3] * lv[3:4], axis=-1, keepdims=True)) + lam_init)
    lane = lax.broadcasted_iota(jnp.int32, (tq, V_DIM), 1)
    zero = jnp.zeros((tq, V_DIM), BF16)

    gs_row = lax.broadcasted_iota(jnp.int32, (V_DIM, V_DIM), 0)
    gs_col = lax.broadcasted_iota(jnp.int32, (V_DIM, V_DIM), 1)
    group_sum = jnp.where(gs_row // HEAD_DIM == gs_col, 1.0, 0.0).astype(BF16)

    def max_sq_norm(x):
        n2 = jnp.dot(x * x, group_sum, preferred_element_type=F32)
        return jnp.max(n2, axis=0, keepdims=True)

    @pl.when(pl.program_id(2) == 0)
    def _():
        bound_sq = jnp.max(functools.reduce(
            jnp.maximum, [max_sq_norm(qall_ref[hd]) * max_sq_norm(k_ref[hd]) for hd in range(n_heads)]))
        bounded_scr[0] = (bound_sq * NORM_MARGIN <= MAX_UNSHIFTED_SCORE ** 2).astype(jnp.int32)

    scores_bounded = bounded_scr[0] == 1

    def attend(n_keys, subtract_max):
        def scores(hd):
            q = q_ref[hd]
            qs = jnp.concatenate([jnp.where(lane < HEAD_DIM, q, zero),
                                  jnp.where(lane >= HEAD_DIM, q, zero)], axis=0)
            return lax.dot_general(qs, k_ref[hd, :n_keys, :], (((1,), (1,)), ((), ())),
                                   preferred_element_type=F32)

        all_scores = [scores(hd) for hd in range(n_heads)]
        for hd in range(n_heads):
            s = all_scores[hd]
            if subtract_max:
                s = s - jnp.max(s, axis=-1, keepdims=True)
            e = jnp.exp2(s)
            inv = 1.0 / jnp.sum(e, axis=-1, keepdims=True)
            w = e[:tq] - e[tq:] * (lam * inv[tq:] / inv[:tq])
            o = jnp.dot(w.astype(BF16), v_ref[hd, :n_keys, :], preferred_element_type=F32) * inv[:tq]
            o = _rms(o, sg_ref[...]) * (1.0 - lam_init)
            o_ref[hd] = o.astype(BF16)

    def attend_guarded(n_keys):
        pl.when(scores_bounded)(lambda: attend(n_keys, subtract_max=False))
        pl.when(jnp.logical_not(scores_bounded))(lambda: attend(n_keys, subtract_max=True))

    if ctx_tile:
        is_ctx = pl.program_id(2) == 0
        pl.when(is_ctx)(lambda: attend_guarded(ctx_len))
        pl.when(jnp.logical_not(is_ctx))(lambda: attend_guarded(k_ref.shape[1]))
    else:
        attend_guarded(k_ref.shape[1])


def _attention(q, k, v, lam_vecs, subln_g, lam_init, j, dims, context_queries):
    b, s, c = dims
    tq = ATTN_TILE
    blocks_per_b = (s + c) // tq
    lat_per_b = s // tq
    if context_queries:
        n_qt, out_rows = blocks_per_b, b * (s + c)
        q_block = lambda bb, jq: bb * blocks_per_b + jq
        o_block = q_block
    else:
        n_qt, out_rows = lat_per_b, b * s
        q_block = lambda bb, jq: bb * blocks_per_b + c // tq + jq
        o_block = lambda bb, jq: bb * lat_per_b + jq
    hb = HEADS_PER_STEP
    kv_spec = pl.BlockSpec((hb, s + c, V_DIM), lambda bb, hg, jq: (hg, bb, 0))
    return pl.pallas_call(
        functools.partial(_attn_kernel, lam_init=lam_init, ctx_len=c, ctx_tile=context_queries),
        grid=(b, N_HEADS // hb, n_qt),
        in_specs=[pl.BlockSpec((hb, tq, V_DIM), lambda bb, hg, jq: (hg, q_block(bb, jq), 0)),
                  kv_spec, kv_spec, kv_spec, _stacked(lam_vecs, (j,)), _stacked(subln_g, (j,))],
        out_specs=pl.BlockSpec((hb, tq, V_DIM), lambda bb, hg, jq: (hg, o_block(bb, jq), 0)),
        out_shape=jax.ShapeDtypeStruct((N_HEADS, out_rows, V_DIM), BF16),
        scratch_shapes=[pltpu.SMEM((1,), jnp.int32)],
        compiler_params=_params(3),
        name="attention",
    )(q, q, k, v, lam_vecs, subln_g)


def _oproj_kernel(o_ref, h_ref, gt_ref, w_ref, out_ref):
    o = jnp.concatenate([o_ref[hd] for hd in range(N_HEADS)], axis=-1)
    y = jnp.dot(o, w_ref[...], preferred_element_type=F32)
    out_ref[...] = h_ref[...] + gt_ref[...] * y


def _oproj(o_hm, h, mods, w_out, layer, j, dims, n_rows, latent_only):
    d = h.shape[1]
    tm = ATTN_TILE
    row_of_tile, _, hm_block = _tile_maps(dims)
    o_block = (lambda t: t) if latent_only else hm_block
    return pl.pallas_call(
        _oproj_kernel,
        grid=(n_rows // tm,),
        in_specs=[
            pl.BlockSpec((N_HEADS, tm, V_DIM), lambda t: (0, o_block(t), 0)),
            pl.BlockSpec((tm, d), lambda t: (t, 0)),
            _mod_spec(layer, 5, d, row_of_tile),
            _stacked(w_out, (j,), single_buffer=True),
        ],
        out_specs=pl.BlockSpec((tm, d), lambda t: (t, 0)),
        out_shape=jax.ShapeDtypeStruct((n_rows, d), F32),
        compiler_params=_params(1),
        name="attn_out",
    )(o_hm, h, mods, w_out)


def _sg_kernel(h_ref, sh_ref, sc_ref, gt_ref, g_ref, win_ref, lng_ref, lnb_ref, ws_ref, bs_ref, wout_ref,
               o_ref):
    tm = h_ref.shape[0]
    e = wout_ref.shape[0]
    gd = e // SG_GROUPS
    h = h_ref[...]
    xb = _modulated_bf16(h, g_ref[...], sh_ref[...], sc_ref[...])
    u = jax.nn.gelu(jnp.dot(xb, win_ref[:, :e], preferred_element_type=F32))
    v = jax.nn.gelu(jnp.dot(xb, win_ref[:, e:], preferred_element_type=F32))
    mu = jnp.mean(v, axis=-1, keepdims=True)
    vc = v - mu
    vn = vc * lax.rsqrt(jnp.mean(vc * vc, axis=-1, keepdims=True) + LN_EPS)
    vb = (vn * lng_ref[...] + lnb_ref[...]).astype(BF16)
    bias = bs_ref[...]
    rows = []
    for n in range(tm // CHUNK):
        r = slice(n * CHUNK, (n + 1) * CHUNK)
        cols = [jnp.dot(ws_ref[gi], vb[r, gi * gd:(gi + 1) * gd], preferred_element_type=F32)
                for gi in range(SG_GROUPS)]
        rows.append(jnp.concatenate(cols, axis=-1) + bias)
    mixed = jnp.concatenate(rows, axis=0)
    y = jnp.dot((u * mixed).astype(BF16), wout_ref[...], preferred_element_type=F32)
    o_ref[...] = h + gt_ref[...] * y


def _sg_mixer(h, mods, norm_g, w_in, ln_g, ln_b, w_s, bias_full, w_out, layer, j, n_rows, row_of_tile):
    d = h.shape[1]
    tm = ROW_TILE
    return pl.pallas_call(
        _sg_kernel,
        grid=(n_rows // tm,),
        in_specs=[
            pl.BlockSpec((tm, d), lambda t: (t, 0)),
            _mod_spec(layer, 3, d, row_of_tile),
            _mod_spec(layer, 4, d, row_of_tile),
            _mod_spec(layer, 5, d, row_of_tile),
            _stacked(norm_g, (layer, 1)),
            _stacked(w_in, (j,), single_buffer=True),
            _stacked(ln_g, (j,)),
            _stacked(ln_b, (j,)),
            _stacked(w_s, (j,)),
            _stacked(bias_full, (j,)),
            _stacked(w_out, (j,), single_buffer=True),
        ],
        out_specs=pl.BlockSpec((tm, d), lambda t: (t, 0)),
        out_shape=jax.ShapeDtypeStruct((n_rows, d), F32),
        compiler_params=_params(1),
        name="sg_mixer",
    )(h, mods, mods, mods, norm_g, w_in, ln_g, ln_b, w_s, bias_full, w_out)


def _rope_tables(s, tm):
    rows_n = s // GRID_W
    row = jnp.repeat(jnp.arange(rows_n), GRID_W)
    col = jnp.tile(jnp.arange(GRID_W), rows_n)
    n_freq = HEAD_DIM // 4
    inv = ROPE_THETA ** (-jnp.arange(n_freq, dtype=F32) / n_freq)
    pos = jnp.stack([row, col], axis=-1).astype(F32)
    ang = pos[:, :, None] * inv
    cos, sin = jnp.cos(ang), jnp.sin(ang)
    cos_l = jnp.broadcast_to(cos[:, None, :, None, :], (s, 2, 2, 2, n_freq)).reshape(s, V_DIM)
    sign = jnp.array([-1.0, 1.0], F32)[None, None, None, :, None]
    sin_l = jnp.broadcast_to(sin[:, None, :, None, :] * sign, (s, 2, 2, 2, n_freq)).reshape(s, V_DIM)
    cos_t = jnp.concatenate([cos_l, jnp.ones((tm, V_DIM), F32)], axis=0)
    sin_t = jnp.concatenate([sin_l, jnp.zeros((tm, V_DIM), F32)], axis=0)
    return cos_t, sin_t


def kernel(x, c, ctx, c_ctx, w_mod, b_mod, norm_g, w_ffn_gu, w_ffn_down, da_w_in, da_w_out, da_lambda,
           da_subln_g, sg_w_in, sg_ln_g, sg_ln_b, sg_w_s, sg_b_s, sg_w_out, final_g):
    b, s, d = x.shape
    cl = ctx.shape[1]
    depth = w_mod.shape[0]
    e = sg_w_out.shape[1]
    assert cl == ATTN_TILE and s % ROW_TILE == 0 and s % GRID_W == 0 and b < MOD_ROWS
    assert (b * cl) % ROW_TILE == 0 and ROW_TILE % CHUNK == 0 and cl % CHUNK == 0
    dims = (b, s, cl)
    n_lat, n_all = b * s, b * (s + cl)

    def row_of_tile(t):
        return jnp.minimum(t // (s // ROW_TILE), b)

    c_all = jnp.zeros((MOD_ROWS, d), F32).at[:b].set(c).at[b].set(c_ctx)
    mods = _mod_table(c_all, w_mod, b_mod).reshape(depth, MOD_ROWS, N_MOD, 1, d)
    cos_t, sin_t = _rope_tables(s, ATTN_TILE)

    norm_g4 = norm_g.reshape(depth, 3, 1, d)
    w_gu = w_ffn_gu.astype(BF16)
    w_down = w_ffn_down.astype(BF16)
    a_in, a_out = da_w_in.astype(BF16), da_w_out.astype(BF16)
    subln_g = da_subln_g.reshape(-1, 1, V_DIM)
    g_in, g_out, g_ws = sg_w_in.astype(BF16), sg_w_out.astype(BF16), sg_w_s.astype(BF16)
    ln_g, ln_b = sg_ln_g.reshape(-1, 1, e), sg_ln_b.reshape(-1, 1, e)
    bias_full = jnp.repeat(jnp.swapaxes(sg_b_s, 1, 2), e // SG_GROUPS, axis=2)

    last_ctx_layer = max(i for i in range(depth) if i % N_MIXERS == 0)
    h = x.reshape(n_lat, d)
    for i in range(depth):
        j = i // N_MIXERS
        rows_in = n_all if i <= last_ctx_layer else n_lat
        rows_out = n_all if i < last_ctx_layer else n_lat
        h_tail = ctx.reshape(b * cl, d) if i == 0 and rows_in == n_all else None
        h = _ffn(h, mods, norm_g4, w_gu, w_down, i, 0, rows_in, row_of_tile, h_tail=h_tail)
        if i % N_MIXERS == 0:
            lam_init = 0.8 - 0.6 * math.exp(-0.3 * i)
            q, k, v = _qkv(h, mods, norm_g4, a_in, cos_t, sin_t, i, j, dims)
            full = i < last_ctx_layer
            o = _attention(q, k, v, da_lambda, subln_g, lam_init, j, dims, context_queries=full)
            h = _oproj(o, h, mods, a_out, i, j, dims, rows_out, latent_only=not full)
        else:
            h = _sg_mixer(h, mods, norm_g4, g_in, ln_g, ln_b, g_ws, bias_full, g_out, i, j, rows_out,
                          row_of_tile)
        h = _ffn(h, mods, norm_g4, w_gu, w_down, i, 1, rows_out, row_of_tile,
                 final_g=final_g if i == depth - 1 else None)
    return h.reshape(b, s, d)
```

```python
import functools
import math

import jax
import jax.numpy as jnp
from jax import lax
from jax.experimental import pallas as pl
from jax.experimental.pallas import tpu as pltpu

F32 = jnp.float32
BF16 = jnp.bfloat16

GRID_W = 64
N_MIXERS = 2
N_HEADS = 8
HEAD_DIM = 64
V_DIM = 2 * HEAD_DIM
ROPE_THETA = 10000.0
CHUNK = 128
SG_GROUPS = 8
N_MOD = 9
RMS_EPS = 1e-6
LN_EPS = 1e-5

V7X_LANES = 128
V7X_VMEM_BYTES = 64 * 1024 * 1024
VMEM_LIMIT_BYTES = V7X_VMEM_BYTES - 8 * 1024 * 1024

MOD_ROWS = 16
ATTN_TILE = 256
ROW_TILE = 512
HEADS_PER_STEP = 4
LOG2_E = math.log2(math.e)
MAX_UNSHIFTED_SCORE = 40.0
NORM_MARGIN = 1.05


def _params(n_axes):
    return pltpu.CompilerParams(dimension_semantics=("arbitrary",) * n_axes,
                                vmem_limit_bytes=VMEM_LIMIT_BYTES)


def _rms(x, g):
    y = x * lax.rsqrt(jnp.mean(x * x, axis=-1, keepdims=True) + RMS_EPS)
    return y * g


def _modulated_bf16(h, g, shift, scale):
    return (_rms(h, g) * (1.0 + scale) + shift).astype(BF16)


def _stacked(arr, idx, single_buffer=False):
    tail = arr.shape[len(idx):]
    index = tuple(idx) + (0,) * len(tail)
    kwargs = dict(pipeline_mode=pl.Buffered(1)) if single_buffer else {}
    return pl.BlockSpec((None,) * len(idx) + tail, lambda *_: index, **kwargs)


def _mod_spec(layer, k, d, row_of_tile):
    return pl.BlockSpec((None, None, None, 1, d), lambda t: (layer, row_of_tile(t), k, 0, 0))


def _mod_kernel(c_ref, w_ref, b_ref, o_ref):
    sc = jax.nn.silu(c_ref[...]).astype(BF16)
    o_ref[...] = jnp.dot(sc, w_ref[...].astype(BF16), preferred_element_type=F32) + b_ref[...]


def _mod_table(c_all, w_mod, b_mod):
    depth, d, nd = w_mod.shape
    tn = d
    return pl.pallas_call(
        _mod_kernel,
        grid=(depth, nd // tn),
        in_specs=[
            pl.BlockSpec((MOD_ROWS, d), lambda i, n: (0, 0)),
            pl.BlockSpec((None, d, tn), lambda i, n: (i, 0, n)),
            pl.BlockSpec((None, 1, tn), lambda i, n: (i, 0, n)),
        ],
        out_specs=pl.BlockSpec((None, MOD_ROWS, tn), lambda i, n: (i, 0, n)),
        out_shape=jax.ShapeDtypeStruct((depth, MOD_ROWS, nd), F32),
        compiler_params=_params(2),
        name="mod_table",
    )(c_all, w_mod, b_mod.reshape(depth, 1, nd))


def _ffn_kernel(*refs, n_first, n_attn, final):
    refs = list(refs)
    o_ref = refs.pop()
    fg_ref = refs.pop() if final else None
    h_ref = refs.pop(0)
    h = h_ref[...]
    if n_first is not None:
        h = jnp.where(pl.program_id(0) < n_first, h, refs.pop(0)[...])
    if n_attn:
        attn_refs = [refs.pop(0) for _ in range(n_attn)]
        mixer_gate_ref, wout_ref = refs.pop(0), refs.pop(0)
        attn = jnp.concatenate(
            [jnp.concatenate([r[hd] for hd in range(N_HEADS)], axis=-1) for r in attn_refs], axis=0)
        h = h + mixer_gate_ref[...] * jnp.dot(attn, wout_ref[...], preferred_element_type=F32)
    sh_ref, sc_ref, gt_ref, g_ref, wgu_ref, wd_ref = refs
    d_ff = wd_ref.shape[0]
    xb = _modulated_bf16(h, g_ref[...], sh_ref[...], sc_ref[...])
    a = jnp.dot(xb, wgu_ref[:, :d_ff], preferred_element_type=F32)
    u = jnp.dot(xb, wgu_ref[:, d_ff:], preferred_element_type=F32)
    act = (jax.nn.silu(a) * u).astype(BF16)
    y = jnp.dot(act, wd_ref[...], preferred_element_type=F32)
    out = h + (0.5 * gt_ref[...]) * y
    if final:
        out = _rms(out, fg_ref[...])
    o_ref[...] = out


def _ffn(h, mods, norm_g, w_gu, w_down, layer, which, n_rows, row_of_tile, final_g=None, h_tail=None,
         attn=None):
    d = h.shape[1]
    tm = ROW_TILE
    k0 = 6 * which
    final = final_g is not None
    n_first = None
    h_specs, h_args = [pl.BlockSpec((tm, d), lambda t: (t, 0))], [h]
    if h_tail is not None:
        n_first = h.shape[0] // tm
        h_specs = [pl.BlockSpec((tm, d), lambda t: (jnp.minimum(t, n_first - 1), 0)),
                   pl.BlockSpec((tm, d), lambda t: (jnp.maximum(t - n_first, 0), 0))]
        h_args = [h, h_tail]
    n_attn = 0
    if attn is not None:
        o_hm, w_out, j, dims, latent_only = attn
        if latent_only:
            h_specs.append(pl.BlockSpec((N_HEADS, tm, V_DIM), lambda t: (0, t, 0)))
            n_attn = 1
        else:
            _, _, hm_block = _tile_maps(dims)
            n_attn = tm // ATTN_TILE
            for part in range(n_attn):
                h_specs.append(pl.BlockSpec((N_HEADS, ATTN_TILE, V_DIM),
                                            lambda t, part=part: (0, hm_block(n_attn * t + part), 0)))
        h_specs += [_mod_spec(layer, 5, d, row_of_tile), _stacked(w_out, (j,), single_buffer=True)]
        h_args += [o_hm] * n_attn + [mods, w_out]
    in_specs = h_specs + [
        _mod_spec(layer, k0, d, row_of_tile),
        _mod_spec(layer, k0 + 1, d, row_of_tile),
        _mod_spec(layer, k0 + 2, d, row_of_tile),
        _stacked(norm_g, (layer, 2 * which)),
        _stacked(w_gu, (layer, which), single_buffer=True),
        _stacked(w_down, (layer, which), single_buffer=True),
    ]
    args = h_args + [mods, mods, mods, norm_g, w_gu, w_down]
    if final:
        in_specs.append(pl.BlockSpec((1, d), lambda t: (0, 0)))
        args.append(final_g.reshape(1, d))
    return pl.pallas_call(
        functools.partial(_ffn_kernel, n_first=n_first, n_attn=n_attn, final=final),
        grid=(n_rows // tm,),
        in_specs=in_specs,
        out_specs=pl.BlockSpec((tm, d), lambda t: (t, 0)),
        out_shape=jax.ShapeDtypeStruct((n_rows, d), F32),
        compiler_params=_params(1),
        name="ffn",
    )(*args)


def _qkv_kernel(h_ref, sh_ref, sc_ref, g_ref, w_ref, cos_ref, sin_ref, q_ref, k_ref, v_ref):
    d = h_ref.shape[1]
    xb = _modulated_bf16(h_ref[...], g_ref[...], sh_ref[...], sc_ref[...])
    cos = cos_ref[...]
    sin = sin_ref[...]
    lane = lax.broadcasted_iota(jnp.int32, cos.shape, 1)
    first_half = (lane % (HEAD_DIM // 2)) < (HEAD_DIM // 4)

    def rope(x):
        partner = jnp.where(first_half,
                            pltpu.roll(x, V7X_LANES - HEAD_DIM // 4, axis=1),
                            pltpu.roll(x, HEAD_DIM // 4, axis=1))
        return x * cos + partner * sin

    q = jnp.dot(xb, w_ref[:, :d], preferred_element_type=F32)
    k = jnp.dot(xb, w_ref[:, d:2 * d], preferred_element_type=F32)
    v = jnp.dot(xb, w_ref[:, 2 * d:], preferred_element_type=F32)
    for hd in range(N_HEADS):
        sl = slice(hd * V_DIM, (hd + 1) * V_DIM)
        q_ref[hd] = (rope(q[:, sl]) * (LOG2_E * HEAD_DIM ** -0.5)).astype(BF16)
        k_ref[hd] = rope(k[:, sl]).astype(BF16)
        v_ref[hd] = v[:, sl].astype(BF16)


def _tile_maps(dims):
    b, s, c = dims
    tm = ATTN_TILE
    n_lat = b * s // tm
    per_b = s // tm
    blocks_per_b = (s + c) // tm

    def row_of_tile(t):
        return jnp.minimum(t // per_b, b)

    def rope_block(t):
        return jnp.where(t < n_lat, t % per_b, per_b)

    def hm_block(t):
        return jnp.where(t < n_lat, (t // per_b) * blocks_per_b + 1 + t % per_b, (t - n_lat) * blocks_per_b)

    return row_of_tile, rope_block, hm_block


def _qkv(h, mods, norm_g, w_in, cos_t, sin_t, layer, j, dims):
    b, s, c = dims
    d = h.shape[1]
    tm = ATTN_TILE
    row_of_tile, rope_block, hm_block = _tile_maps(dims)
    hm = jax.ShapeDtypeStruct((N_HEADS, b * (s + c), V_DIM), BF16)
    hm_spec = pl.BlockSpec((N_HEADS, tm, V_DIM), lambda t: (0, hm_block(t), 0))
    return pl.pallas_call(
        _qkv_kernel,
        grid=(b * (s + c) // tm,),
        in_specs=[
            pl.BlockSpec((tm, d), lambda t: (t, 0)),
            _mod_spec(layer, 3, d, row_of_tile),
            _mod_spec(layer, 4, d, row_of_tile),
            _stacked(norm_g, (layer, 1)),
            _stacked(w_in, (j,), single_buffer=True),
            pl.BlockSpec((tm, V_DIM), lambda t: (rope_block(t), 0)),
            pl.BlockSpec((tm, V_DIM), lambda t: (rope_block(t), 0)),
        ],
        out_specs=[hm_spec, hm_spec, hm_spec],
        out_shape=[hm, hm, hm],
        compiler_params=_params(1),
        name="qkv_rope",
    )(h, mods, mods, norm_g, w_in, cos_t, sin_t)


def _attn_kernel(q_ref, qall_ref, k_ref, v_ref, lam_ref, sg_ref, o_ref, bounded_scr, *, lam_init, ctx_len,
                 ctx_tile):
    n_heads, tq, _ = q_ref.shape
    lv = lam_ref[...]
    lam = (jnp.exp(jnp.sum(lv[0:1] * lv[1:2], axis=-1, keepdims=True))
           - jnp.exp(jnp.sum(lv[2:3] * lv[3:4], axis=-1, keepdims=True)) + lam_init)
    lane = lax.broadcasted_iota(jnp.int32, (tq, V_DIM), 1)
    zero = jnp.zeros((tq, V_DIM), BF16)

    gs_row = lax.broadcasted_iota(jnp.int32, (V_DIM, V_DIM), 0)
    gs_col = lax.broadcasted_iota(jnp.int32, (V_DIM, V_DIM), 1)
    group_sum = jnp.where(gs_row // HEAD_DIM == gs_col, 1.0, 0.0).astype(BF16)

    def max_sq_norm(x):
        n2 = jnp.dot(x * x, group_sum, preferred_element_type=F32)
        return jnp.max(n2, axis=0, keepdims=True)

    @pl.when(pl.program_id(2) == 0)
    def _():
        bound_sq = jnp.max(functools.reduce(
            jnp.maximum, [max_sq_norm(qall_ref[hd]) * max_sq_norm(k_ref[hd]) for hd in range(n_heads)]))
        bounded_scr[0] = (bound_sq * NORM_MARGIN <= MAX_UNSHIFTED_SCORE ** 2).astype(jnp.int32)

    scores_bounded = bounded_scr[0] == 1

    def attend(n_keys, subtract_max):
        def scores(hd):
            q = q_ref[hd]
            qs = jnp.concatenate([jnp.where(lane < HEAD_DIM, q, zero),
                                  jnp.where(lane >= HEAD_DIM, q, zero)], axis=0)
            return lax.dot_general(qs, k_ref[hd, :n_keys, :], (((1,), (1,)), ((), ())),
                                   preferred_element_type=F32)

        all_scores = [scores(hd) for hd in range(n_heads)]
        for hd in range(n_heads):
            s = all_scores[hd]
            if subtract_max:
                s = s - jnp.max(s, axis=-1, keepdims=True)
            e = jnp.exp2(s)
            inv = 1.0 / jnp.sum(e, axis=-1, keepdims=True)
            w = e[:tq] - e[tq:] * (lam * inv[tq:] / inv[:tq])
            o = jnp.dot(w.astype(BF16), v_ref[hd, :n_keys, :], preferred_element_type=F32) * inv[:tq]
            o = _rms(o, sg_ref[...]) * (1.0 - lam_init)
            o_ref[hd] = o.astype(BF16)

    def attend_guarded(n_keys):
        pl.when(scores_bounded)(lambda: attend(n_keys, subtract_max=False))
        pl.when(jnp.logical_not(scores_bounded))(lambda: attend(n_keys, subtract_max=True))

    if ctx_tile:
        is_ctx = pl.program_id(2) == 0
        pl.when(is_ctx)(lambda: attend_guarded(ctx_len))
        pl.when(jnp.logical_not(is_ctx))(lambda: attend_guarded(k_ref.shape[1]))
    else:
        attend_guarded(k_ref.shape[1])


def _attention(q, k, v, lam_vecs, subln_g, lam_init, j, dims, context_queries):
    b, s, c = dims
    tq = ATTN_TILE
    blocks_per_b = (s + c) // tq
    lat_per_b = s // tq
    if context_queries:
        n_qt, out_rows = blocks_per_b, b * (s + c)
        q_block = lambda bb, jq: bb * blocks_per_b + jq
        o_block = q_block
    else:
        n_qt, out_rows = lat_per_b, b * s
        q_block = lambda bb, jq: bb * blocks_per_b + c // tq + jq
        o_block = lambda bb, jq: bb * lat_per_b + jq
    hb = HEADS_PER_STEP
    kv_spec = pl.BlockSpec((hb, s + c, V_DIM), lambda bb, hg, jq: (hg, bb, 0))
    return pl.pallas_call(
        functools.partial(_attn_kernel, lam_init=lam_init, ctx_len=c, ctx_tile=context_queries),
        grid=(b, N_HEADS // hb, n_qt),
        in_specs=[pl.BlockSpec((hb, tq, V_DIM), lambda bb, hg, jq: (hg, q_block(bb, jq), 0)),
                  kv_spec, kv_spec, kv_spec, _stacked(lam_vecs, (j,)), _stacked(subln_g, (j,))],
        out_specs=pl.BlockSpec((hb, tq, V_DIM), lambda bb, hg, jq: (hg, o_block(bb, jq), 0)),
        out_shape=jax.ShapeDtypeStruct((N_HEADS, out_rows, V_DIM), BF16),
        scratch_shapes=[pltpu.SMEM((1,), jnp.int32)],
        compiler_params=_params(3),
        name="attention",
    )(q, q, k, v, lam_vecs, subln_g)


def _sg_kernel(h_ref, sh_ref, sc_ref, gt_ref, g_ref, win_ref, lng_ref, lnb_ref, ws_ref, bs_ref, wout_ref,
               o_ref):
    tm = h_ref.shape[0]
    e = wout_ref.shape[0]
    gd = e // SG_GROUPS
    h = h_ref[...]
    xb = _modulated_bf16(h, g_ref[...], sh_ref[...], sc_ref[...])
    u = jax.nn.gelu(jnp.dot(xb, win_ref[:, :e], preferred_element_type=F32))
    v = jax.nn.gelu(jnp.dot(xb, win_ref[:, e:], preferred_element_type=F32))
    mu = jnp.mean(v, axis=-1, keepdims=True)
    vc = v - mu
    vn = vc * lax.rsqrt(jnp.mean(vc * vc, axis=-1, keepdims=True) + LN_EPS)
    vb = (vn * lng_ref[...] + lnb_ref[...]).astype(BF16)
    bias = bs_ref[...]
    rows = []
    for n in range(tm // CHUNK):
        r = slice(n * CHUNK, (n + 1) * CHUNK)
        cols = [jnp.dot(ws_ref[gi], vb[r, gi * gd:(gi + 1) * gd], preferred_element_type=F32)
                for gi in range(SG_GROUPS)]
        rows.append(jnp.concatenate(cols, axis=-1) + bias)
    mixed = jnp.concatenate(rows, axis=0)
    y = jnp.dot((u * mixed).astype(BF16), wout_ref[...], preferred_element_type=F32)
    o_ref[...] = h + gt_ref[...] * y


def _sg_mixer(h, mods, norm_g, w_in, ln_g, ln_b, w_s, bias_full, w_out, layer, j, n_rows, row_of_tile):
    d = h.shape[1]
    tm = ROW_TILE
    return pl.pallas_call(
        _sg_kernel,
        grid=(n_rows // tm,),
        in_specs=[
            pl.BlockSpec((tm, d), lambda t: (t, 0)),
            _mod_spec(layer, 3, d, row_of_tile),
            _mod_spec(layer, 4, d, row_of_tile),
            _mod_spec(layer, 5, d, row_of_tile),
            _stacked(norm_g, (layer, 1)),
            _stacked(w_in, (j,), single_buffer=True),
            _stacked(ln_g, (j,)),
            _stacked(ln_b, (j,)),
            _stacked(w_s, (j,)),
            _stacked(bias_full, (j,)),
            _stacked(w_out, (j,), single_buffer=True),
        ],
        out_specs=pl.BlockSpec((tm, d), lambda t: (t, 0)),
        out_shape=jax.ShapeDtypeStruct((n_rows, d), F32),
        compiler_params=_params(1),
        name="sg_mixer",
    )(h, mods, mods, mods, norm_g, w_in, ln_g, ln_b, w_s, bias_full, w_out)


def _rope_tables(s, tm):
    rows_n = s // GRID_W
    row = jnp.repeat(jnp.arange(rows_n), GRID_W)
    col = jnp.tile(jnp.arange(GRID_W), rows_n)
    n_freq = HEAD_DIM // 4
    inv = ROPE_THETA ** (-jnp.arange(n_freq, dtype=F32) / n_freq)
    pos = jnp.stack([row, col], axis=-1).astype(F32)
    ang = pos[:, :, None] * inv
    cos, sin = jnp.cos(ang), jnp.sin(ang)
    cos_l = jnp.broadcast_to(cos[:, None, :, None, :], (s, 2, 2, 2, n_freq)).reshape(s, V_DIM)
    sign = jnp.array([-1.0, 1.0], F32)[None, None, None, :, None]
    sin_l = jnp.broadcast_to(sin[:, None, :, None, :] * sign, (s, 2, 2, 2, n_freq)).reshape(s, V_DIM)
    cos_t = jnp.concatenate([cos_l, jnp.ones((tm, V_DIM), F32)], axis=0)
    sin_t = jnp.concatenate([sin_l, jnp.zeros((tm, V_DIM), F32)], axis=0)
    return cos_t, sin_t


def kernel(x, c, ctx, c_ctx, w_mod, b_mod, norm_g, w_ffn_gu, w_ffn_down, da_w_in, da_w_out, da_lambda,
           da_subln_g, sg_w_in, sg_ln_g, sg_ln_b, sg_w_s, sg_b_s, sg_w_out, final_g):
    b, s, d = x.shape
    cl = ctx.shape[1]
    depth = w_mod.shape[0]
    e = sg_w_out.shape[1]
    assert cl == ATTN_TILE and s % ROW_TILE == 0 and s % GRID_W == 0 and b < MOD_ROWS
    assert (b * cl) % ROW_TILE == 0 and ROW_TILE % CHUNK == 0 and cl % CHUNK == 0
    dims = (b, s, cl)
    n_lat, n_all = b * s, b * (s + cl)

    def row_of_tile(t):
        return jnp.minimum(t // (s // ROW_TILE), b)

    c_all = jnp.zeros((MOD_ROWS, d), F32).at[:b].set(c).at[b].set(c_ctx)
    mods = _mod_table(c_all, w_mod, b_mod).reshape(depth, MOD_ROWS, N_MOD, 1, d)
    cos_t, sin_t = _rope_tables(s, ATTN_TILE)

    norm_g4 = norm_g.reshape(depth, 3, 1, d)
    w_gu = w_ffn_gu.astype(BF16)
    w_down = w_ffn_down.astype(BF16)
    a_in, a_out = da_w_in.astype(BF16), da_w_out.astype(BF16)
    subln_g = da_subln_g.reshape(-1, 1, V_DIM)
    g_in, g_out, g_ws = sg_w_in.astype(BF16), sg_w_out.astype(BF16), sg_w_s.astype(BF16)
    ln_g, ln_b = sg_ln_g.reshape(-1, 1, e), sg_ln_b.reshape(-1, 1, e)
    bias_full = jnp.repeat(jnp.swapaxes(sg_b_s, 1, 2), e // SG_GROUPS, axis=2)

    last_ctx_layer = max(i for i in range(depth) if i % N_MIXERS == 0)
    h = x.reshape(n_lat, d)
    for i in range(depth):
        j = i // N_MIXERS
        rows_in = n_all if i <= last_ctx_layer else n_lat
        rows_out = n_all if i < last_ctx_layer else n_lat
        h_tail = ctx.reshape(b * cl, d) if i == 0 and rows_in == n_all else None
        h = _ffn(h, mods, norm_g4, w_gu, w_down, i, 0, rows_in, row_of_tile, h_tail=h_tail)
        attn = None
        if i % N_MIXERS == 0:
            lam_init = 0.8 - 0.6 * math.exp(-0.3 * i)
            q, k, v = _qkv(h, mods, norm_g4, a_in, cos_t, sin_t, i, j, dims)
            full = i < last_ctx_layer
            o = _attention(q, k, v, da_lambda, subln_g, lam_init, j, dims, context_queries=full)
            attn = (o, a_out, j, dims, not full)
        else:
            h = _sg_mixer(h, mods, norm_g4, g_in, ln_g, ln_b, g_ws, bias_full, g_out, i, j, rows_out,
                          row_of_tile)
        h = _ffn(h, mods, norm_g4, w_gu, w_down, i, 1, rows_out, row_of_tile,
                 final_g=final_g if i == depth - 1 else None, attn=attn)
    return h.reshape(b, s, d)
```

```python
import functools
import math

import jax
import jax.numpy as jnp
from jax import lax
from jax.experimental import pallas as pl
from jax.experimental.pallas import tpu as pltpu

F32 = jnp.float32
BF16 = jnp.bfloat16

GRID_W = 64
N_MIXERS = 2
N_HEADS = 8
HEAD_DIM = 64
V_DIM = 2 * HEAD_DIM
ROPE_THETA = 10000.0
CHUNK = 128
SG_GROUPS = 8
N_MOD = 9
RMS_EPS = 1e-6
LN_EPS = 1e-5

V7X_LANES = 128
V7X_BF16_SUBLANES = 16
V7X_VMEM_BYTES = 64 * 1024 * 1024
VMEM_LIMIT_BYTES = V7X_VMEM_BYTES - 8 * 1024 * 1024

MOD_ROWS = 16
ATTN_TILE = 256
ROW_TILE = 512
HEADS_PER_STEP = 4
LOG2_E = math.log2(math.e)
MAX_UNSHIFTED_SCORE = 40.0
NORM_MARGIN = 1.05


def _params(n_axes):
    return pltpu.CompilerParams(dimension_semantics=("arbitrary",) * n_axes,
                                vmem_limit_bytes=VMEM_LIMIT_BYTES)


def _rms(x, g):
    y = x * lax.rsqrt(jnp.mean(x * x, axis=-1, keepdims=True) + RMS_EPS)
    return y * g


def _modulated_bf16(h, g, shift, scale):
    return (_rms(h, g) * (1.0 + scale) + shift).astype(BF16)


def _stacked(arr, idx, single_buffer=False):
    tail = arr.shape[len(idx):]
    index = tuple(idx) + (0,) * len(tail)
    kwargs = dict(pipeline_mode=pl.Buffered(1)) if single_buffer else {}
    return pl.BlockSpec((None,) * len(idx) + tail, lambda *_: index, **kwargs)


def _mod_spec(layer, k, d, row_of_tile):
    return pl.BlockSpec((None, None, None, 1, d), lambda t: (layer, row_of_tile(t), k, 0, 0))


def _mod_kernel(c_ref, w_ref, b_ref, o_ref):
    sc = jax.nn.silu(c_ref[...]).astype(BF16)
    o_ref[...] = jnp.dot(sc, w_ref[...].astype(BF16), preferred_element_type=F32) + b_ref[...]


def _mod_table(c_all, w_mod, b_mod):
    depth, d, nd = w_mod.shape
    tn = d
    return pl.pallas_call(
        _mod_kernel,
        grid=(depth, nd // tn),
        in_specs=[
            pl.BlockSpec((MOD_ROWS, d), lambda i, n: (0, 0)),
            pl.BlockSpec((None, d, tn), lambda i, n: (i, 0, n)),
            pl.BlockSpec((None, 1, tn), lambda i, n: (i, 0, n)),
        ],
        out_specs=pl.BlockSpec((None, MOD_ROWS, tn), lambda i, n: (i, 0, n)),
        out_shape=jax.ShapeDtypeStruct((depth, MOD_ROWS, nd), F32),
        compiler_params=_params(2),
        name="mod_table",
    )(c_all, w_mod, b_mod.reshape(depth, 1, nd))


def _ffn_kernel(*refs, n_first, n_attn, final, n_cast):
    refs = list(refs)
    cast_out = [refs.pop() for _ in range(n_cast)][::-1]
    o_ref = refs.pop()
    cast_in = [refs.pop() for _ in range(n_cast)][::-1]
    for src_ref, dst_ref in zip(cast_in, cast_out):
        dst_ref[...] = src_ref[...].astype(BF16)
    fg_ref = refs.pop() if final else None
    h_ref = refs.pop(0)
    h = h_ref[...]
    if n_first is not None:
        h = jnp.where(pl.program_id(0) < n_first, h, refs.pop(0)[...])
    if n_attn:
        attn_refs = [refs.pop(0) for _ in range(n_attn)]
        mixer_gate_ref, wout_ref = refs.pop(0), refs.pop(0)
        attn = jnp.concatenate(
            [jnp.concatenate([r[hd] for hd in range(N_HEADS)], axis=-1) for r in attn_refs], axis=0)
        h = h + mixer_gate_ref[...] * jnp.dot(attn, wout_ref[...], preferred_element_type=F32)
    sh_ref, sc_ref, gt_ref, g_ref, wgu_ref, wd_ref = refs
    d_ff = wd_ref.shape[0]
    xb = _modulated_bf16(h, g_ref[...], sh_ref[...], sc_ref[...])
    a = jnp.dot(xb, wgu_ref[:, :d_ff], preferred_element_type=F32)
    u = jnp.dot(xb, wgu_ref[:, d_ff:], preferred_element_type=F32)
    act = (jax.nn.silu(a) * u).astype(BF16)
    y = jnp.dot(act, wd_ref[...], preferred_element_type=F32)
    out = h + (0.5 * gt_ref[...]) * y
    if final:
        out = _rms(out, fg_ref[...])
    o_ref[...] = out


def _ffn(h, mods, norm_g, w_gu, w_down, layer, which, n_rows, row_of_tile, final_g=None, h_tail=None,
         attn=None, cast_next=()):
    d = h.shape[1]
    tm = ROW_TILE
    k0 = 6 * which
    final = final_g is not None
    n_first = None
    h_specs, h_args = [pl.BlockSpec((tm, d), lambda t: (t, 0))], [h]
    if h_tail is not None:
        n_first = h.shape[0] // tm
        h_specs = [pl.BlockSpec((tm, d), lambda t: (jnp.minimum(t, n_first - 1), 0)),
                   pl.BlockSpec((tm, d), lambda t: (jnp.maximum(t - n_first, 0), 0))]
        h_args = [h, h_tail]
    n_attn = 0
    if attn is not None:
        o_hm, w_out, dims, latent_only = attn
        if latent_only:
            h_specs.append(pl.BlockSpec((N_HEADS, tm, V_DIM), lambda t: (0, t, 0)))
            n_attn = 1
        else:
            _, _, hm_block = _tile_maps(dims)
            n_attn = tm // ATTN_TILE
            for part in range(n_attn):
                h_specs.append(pl.BlockSpec((N_HEADS, ATTN_TILE, V_DIM),
                                            lambda t, part=part: (0, hm_block(n_attn * t + part), 0)))
        h_specs += [_mod_spec(layer, 5, d, row_of_tile), _stacked(w_out, (), single_buffer=True)]
        h_args += [o_hm] * n_attn + [mods, w_out]
    in_specs = h_specs + [
        _mod_spec(layer, k0, d, row_of_tile),
        _mod_spec(layer, k0 + 1, d, row_of_tile),
        _mod_spec(layer, k0 + 2, d, row_of_tile),
        _stacked(norm_g, (layer, 2 * which)),
        _stacked(w_gu, (), single_buffer=True),
        _stacked(w_down, (), single_buffer=True),
    ]
    args = h_args + [mods, mods, mods, norm_g, w_gu, w_down]
    if final:
        in_specs.append(pl.BlockSpec((1, d), lambda t: (0, 0)))
        args.append(final_g.reshape(1, d))
    out_specs = [pl.BlockSpec((tm, d), lambda t: (t, 0))]
    out_shape = [jax.ShapeDtypeStruct((n_rows, d), F32)]
    n_steps = n_rows // tm
    for w32, idx in cast_next:
        rows, cols = w32.shape[len(idx):]
        n_blk = max(n for n in range(1, n_steps + 1)
                    if rows % n == 0 and (rows // n) % V7X_BF16_SUBLANES == 0)
        blk = rows // n_blk
        in_specs.append(pl.BlockSpec((None,) * len(idx) + (blk, cols),
                                     lambda t, idx=idx, n_blk=n_blk: idx + (jnp.minimum(t, n_blk - 1), 0)))
        args.append(w32)
        out_specs.append(pl.BlockSpec((blk, cols), lambda t, n_blk=n_blk: (jnp.minimum(t, n_blk - 1), 0)))
        out_shape.append(jax.ShapeDtypeStruct((rows, cols), BF16))
    return pl.pallas_call(
        functools.partial(_ffn_kernel, n_first=n_first, n_attn=n_attn, final=final, n_cast=len(cast_next)),
        grid=(n_rows // tm,),
        in_specs=in_specs,
        out_specs=out_specs,
        out_shape=out_shape,
        compiler_params=_params(1),
        name="ffn",
    )(*args)


def _qkv_kernel(h_ref, sh_ref, sc_ref, g_ref, w_ref, cos_ref, sin_ref, q_ref, k_ref, v_ref):
    d = h_ref.shape[1]
    xb = _modulated_bf16(h_ref[...], g_ref[...], sh_ref[...], sc_ref[...])
    cos = cos_ref[...]
    sin = sin_ref[...]
    lane = lax.broadcasted_iota(jnp.int32, cos.shape, 1)
    first_half = (lane % (HEAD_DIM // 2)) < (HEAD_DIM // 4)

    def rope(x):
        partner = jnp.where(first_half,
                            pltpu.roll(x, V7X_LANES - HEAD_DIM // 4, axis=1),
                            pltpu.roll(x, HEAD_DIM // 4, axis=1))
        return x * cos + partner * sin

    q = jnp.dot(xb, w_ref[:, :d], preferred_element_type=F32)
    k = jnp.dot(xb, w_ref[:, d:2 * d], preferred_element_type=F32)
    v = jnp.dot(xb, w_ref[:, 2 * d:], preferred_element_type=F32)
    for hd in range(N_HEADS):
        sl = slice(hd * V_DIM, (hd + 1) * V_DIM)
        q_ref[hd] = (rope(q[:, sl]) * (LOG2_E * HEAD_DIM ** -0.5)).astype(BF16)
        k_ref[hd] = rope(k[:, sl]).astype(BF16)
        v_ref[hd] = v[:, sl].astype(BF16)


def _tile_maps(dims):
    b, s, c = dims
    tm = ATTN_TILE
    n_lat = b * s // tm
    per_b = s // tm
    blocks_per_b = (s + c) // tm

    def row_of_tile(t):
        return jnp.minimum(t // per_b, b)

    def rope_block(t):
        return jnp.where(t < n_lat, t % per_b, per_b)

    def hm_block(t):
        return jnp.where(t < n_lat, (t // per_b) * blocks_per_b + 1 + t % per_b, (t - n_lat) * blocks_per_b)

    return row_of_tile, rope_block, hm_block


def _qkv(h, mods, norm_g, w_in, cos_t, sin_t, layer, dims):
    b, s, c = dims
    d = h.shape[1]
    tm = ATTN_TILE
    row_of_tile, rope_block, hm_block = _tile_maps(dims)
    hm = jax.ShapeDtypeStruct((N_HEADS, b * (s + c), V_DIM), BF16)
    hm_spec = pl.BlockSpec((N_HEADS, tm, V_DIM), lambda t: (0, hm_block(t), 0))
    return pl.pallas_call(
        _qkv_kernel,
        grid=(b * (s + c) // tm,),
        in_specs=[
            pl.BlockSpec((tm, d), lambda t: (t, 0)),
            _mod_spec(layer, 3, d, row_of_tile),
            _mod_spec(layer, 4, d, row_of_tile),
            _stacked(norm_g, (layer, 1)),
            _stacked(w_in, (), single_buffer=True),
            pl.BlockSpec((tm, V_DIM), lambda t: (rope_block(t), 0)),
            pl.BlockSpec((tm, V_DIM), lambda t: (rope_block(t), 0)),
        ],
        out_specs=[hm_spec, hm_spec, hm_spec],
        out_shape=[hm, hm, hm],
        compiler_params=_params(1),
        name="qkv_rope",
    )(h, mods, mods, norm_g, w_in, cos_t, sin_t)


def _attn_kernel(q_ref, qall_ref, k_ref, v_ref, lam_ref, sg_ref, o_ref, bounded_scr, *, lam_init, ctx_len,
                 ctx_tile):
    n_heads, tq, _ = q_ref.shape
    lv = lam_ref[...]
    lam = (jnp.exp(jnp.sum(lv[0:1] * lv[1:2], axis=-1, keepdims=True))
           - jnp.exp(jnp.sum(lv[2:3] * lv[3:4], axis=-1, keepdims=True)) + lam_init)
    lane = lax.broadcasted_iota(jnp.int32, (tq, V_DIM), 1)
    zero = jnp.zeros((tq, V_DIM), BF16)

    gs_row = lax.broadcasted_iota(jnp.int32, (V_DIM, V_DIM), 0)
    gs_col = lax.broadcasted_iota(jnp.int32, (V_DIM, V_DIM), 1)
    group_sum = jnp.where(gs_row // HEAD_DIM == gs_col, 1.0, 0.0).astype(BF16)

    def max_sq_norm(x):
        n2 = jnp.dot(x * x, group_sum, preferred_element_type=F32)
        return jnp.max(n2, axis=0, keepdims=True)

    @pl.when(pl.program_id(2) == 0)
    def _():
        bound_sq = jnp.max(functools.reduce(
            jnp.maximum, [max_sq_norm(qall_ref[hd]) * max_sq_norm(k_ref[hd]) for hd in range(n_heads)]))
        bounded_scr[0] = (bound_sq * NORM_MARGIN <= MAX_UNSHIFTED_SCORE ** 2).astype(jnp.int32)

    scores_bounded = bounded_scr[0] == 1

    def attend(n_keys, subtract_max):
        def scores(hd):
            q = q_ref[hd]
            qs = jnp.concatenate([jnp.where(lane < HEAD_DIM, q, zero),
                                  jnp.where(lane >= HEAD_DIM, q, zero)], axis=0)
            return lax.dot_general(qs, k_ref[hd, :n_keys, :], (((1,), (1,)), ((), ())),
                                   preferred_element_type=F32)

        all_scores = [scores(hd) for hd in range(n_heads)]
        for hd in range(n_heads):
            s = all_scores[hd]
            if subtract_max:
                s = s - jnp.max(s, axis=-1, keepdims=True)
            e = jnp.exp2(s)
            inv = 1.0 / jnp.sum(e, axis=-1, keepdims=True)
            w = e[:tq] - e[tq:] * (lam * inv[tq:] / inv[:tq])
            o = jnp.dot(w.astype(BF16), v_ref[hd, :n_keys, :], preferred_element_type=F32) * inv[:tq]
            o = _rms(o, sg_ref[...]) * (1.0 - lam_init)
            o_ref[hd] = o.astype(BF16)

    def attend_guarded(n_keys):
        pl.when(scores_bounded)(lambda: attend(n_keys, subtract_max=False))
        pl.when(jnp.logical_not(scores_bounded))(lambda: attend(n_keys, subtract_max=True))

    if ctx_tile:
        is_ctx = pl.program_id(2) == 0
        pl.when(is_ctx)(lambda: attend_guarded(ctx_len))
        pl.when(jnp.logical_not(is_ctx))(lambda: attend_guarded(k_ref.shape[1]))
    else:
        attend_guarded(k_ref.shape[1])


def _attention(q, k, v, lam_vecs, subln_g, lam_init, j, dims, context_queries):
    b, s, c = dims
    tq = ATTN_TILE
    blocks_per_b = (s + c) // tq
    lat_per_b = s // tq
    if context_queries:
        n_qt, out_rows = blocks_per_b, b * (s + c)
        q_block = lambda bb, jq: bb * blocks_per_b + jq
        o_block = q_block
    else:
        n_qt, out_rows = lat_per_b, b * s
        q_block = lambda bb, jq: bb * blocks_per_b + c // tq + jq
        o_block = lambda bb, jq: bb * lat_per_b + jq
    hb = HEADS_PER_STEP
    kv_spec = pl.BlockSpec((hb, s + c, V_DIM), lambda bb, hg, jq: (hg, bb, 0))
    return pl.pallas_call(
        functools.partial(_attn_kernel, lam_init=lam_init, ctx_len=c, ctx_tile=context_queries),
        grid=(b, N_HEADS // hb, n_qt),
        in_specs=[pl.BlockSpec((hb, tq, V_DIM), lambda bb, hg, jq: (hg, q_block(bb, jq), 0)),
                  kv_spec, kv_spec, kv_spec, _stacked(lam_vecs, (j,)), _stacked(subln_g, (j,))],
        out_specs=pl.BlockSpec((hb, tq, V_DIM), lambda bb, hg, jq: (hg, o_block(bb, jq), 0)),
        out_shape=jax.ShapeDtypeStruct((N_HEADS, out_rows, V_DIM), BF16),
        scratch_shapes=[pltpu.SMEM((1,), jnp.int32)],
        compiler_params=_params(3),
        name="attention",
    )(q, q, k, v, lam_vecs, subln_g)


def _sg_kernel(h_ref, sh_ref, sc_ref, gt_ref, g_ref, win_ref, lng_ref, lnb_ref, ws_ref, bs_ref, wout_ref,
               o_ref):
    tm = h_ref.shape[0]
    e = wout_ref.shape[0]
    gd = e // SG_GROUPS
    h = h_ref[...]
    xb = _modulated_bf16(h, g_ref[...], sh_ref[...], sc_ref[...])
    u = jax.nn.gelu(jnp.dot(xb, win_ref[:, :e], preferred_element_type=F32))
    v = jax.nn.gelu(jnp.dot(xb, win_ref[:, e:], preferred_element_type=F32))
    mu = jnp.mean(v, axis=-1, keepdims=True)
    vc = v - mu
    vn = vc * lax.rsqrt(jnp.mean(vc * vc, axis=-1, keepdims=True) + LN_EPS)
    vb = (vn * lng_ref[...] + lnb_ref[...]).astype(BF16)
    bias = bs_ref[...]
    rows = []
    for n in range(tm // CHUNK):
        r = slice(n * CHUNK, (n + 1) * CHUNK)
        cols = [jnp.dot(ws_ref[gi], vb[r, gi * gd:(gi + 1) * gd], preferred_element_type=F32)
                for gi in range(SG_GROUPS)]
        rows.append(jnp.concatenate(cols, axis=-1) + bias)
    mixed = jnp.concatenate(rows, axis=0)
    y = jnp.dot((u * mixed).astype(BF16), wout_ref[...], preferred_element_type=F32)
    o_ref[...] = h + gt_ref[...] * y


def _sg_mixer(h, mods, norm_g, w_in, ln_g, ln_b, w_s, bias_full, w_out, layer, j, n_rows, row_of_tile):
    d = h.shape[1]
    tm = ROW_TILE
    return pl.pallas_call(
        _sg_kernel,
        grid=(n_rows // tm,),
        in_specs=[
            pl.BlockSpec((tm, d), lambda t: (t, 0)),
            _mod_spec(layer, 3, d, row_of_tile),
            _mod_spec(layer, 4, d, row_of_tile),
            _mod_spec(layer, 5, d, row_of_tile),
            _stacked(norm_g, (layer, 1)),
            _stacked(w_in, (), single_buffer=True),
            _stacked(ln_g, (j,)),
            _stacked(ln_b, (j,)),
            _stacked(w_s, (j,)),
            _stacked(bias_full, (j,)),
            _stacked(w_out, (), single_buffer=True),
        ],
        out_specs=pl.BlockSpec((tm, d), lambda t: (t, 0)),
        out_shape=jax.ShapeDtypeStruct((n_rows, d), F32),
        compiler_params=_params(1),
        name="sg_mixer",
    )(h, mods, mods, mods, norm_g, w_in, ln_g, ln_b, w_s, bias_full, w_out)


def _rope_tables(s, tm):
    rows_n = s // GRID_W
    row = jnp.repeat(jnp.arange(rows_n), GRID_W)
    col = jnp.tile(jnp.arange(GRID_W), rows_n)
    n_freq = HEAD_DIM // 4
    inv = ROPE_THETA ** (-jnp.arange(n_freq, dtype=F32) / n_freq)
    pos = jnp.stack([row, col], axis=-1).astype(F32)
    ang = pos[:, :, None] * inv
    cos, sin = jnp.cos(ang), jnp.sin(ang)
    cos_l = jnp.broadcast_to(cos[:, None, :, None, :], (s, 2, 2, 2, n_freq)).reshape(s, V_DIM)
    sign = jnp.array([-1.0, 1.0], F32)[None, None, None, :, None]
    sin_l = jnp.broadcast_to(sin[:, None, :, None, :] * sign, (s, 2, 2, 2, n_freq)).reshape(s, V_DIM)
    cos_t = jnp.concatenate([cos_l, jnp.ones((tm, V_DIM), F32)], axis=0)
    sin_t = jnp.concatenate([sin_l, jnp.zeros((tm, V_DIM), F32)], axis=0)
    return cos_t, sin_t


def kernel(x, c, ctx, c_ctx, w_mod, b_mod, norm_g, w_ffn_gu, w_ffn_down, da_w_in, da_w_out, da_lambda,
           da_subln_g, sg_w_in, sg_ln_g, sg_ln_b, sg_w_s, sg_b_s, sg_w_out, final_g):
    b, s, d = x.shape
    cl = ctx.shape[1]
    depth = w_mod.shape[0]
    e = sg_w_out.shape[1]
    assert cl == ATTN_TILE and s % ROW_TILE == 0 and s % GRID_W == 0 and b < MOD_ROWS
    assert (b * cl) % ROW_TILE == 0 and ROW_TILE % CHUNK == 0 and cl % CHUNK == 0
    dims = (b, s, cl)
    n_lat, n_all = b * s, b * (s + cl)

    def row_of_tile(t):
        return jnp.minimum(t // (s // ROW_TILE), b)

    c_all = jnp.zeros((MOD_ROWS, d), F32).at[:b].set(c).at[b].set(c_ctx)
    mods = _mod_table(c_all, w_mod, b_mod).reshape(depth, MOD_ROWS, N_MOD, 1, d)
    cos_t, sin_t = _rope_tables(s, ATTN_TILE)

    norm_g4 = norm_g.reshape(depth, 3, 1, d)
    ffn_w = [w_ffn_gu[0, 0].astype(BF16), w_ffn_down[0, 0].astype(BF16)]

    def mixer_weights(i):
        w_in, w_out = (da_w_in, da_w_out) if i % N_MIXERS == 0 else (sg_w_in, sg_w_out)
        return (w_in, (i // N_MIXERS,)), (w_out, (i // N_MIXERS,))

    mix_w = [w[idx].astype(BF16) for w, idx in mixer_weights(0)]
    g_ws = sg_w_s.astype(BF16)
    subln_g = da_subln_g.reshape(-1, 1, V_DIM)
    ln_g, ln_b = sg_ln_g.reshape(-1, 1, e), sg_ln_b.reshape(-1, 1, e)
    bias_full = jnp.repeat(jnp.swapaxes(sg_b_s, 1, 2), e // SG_GROUPS, axis=2)

    last_ctx_layer = max(i for i in range(depth) if i % N_MIXERS == 0)
    h = x.reshape(n_lat, d)
    for i in range(depth):
        j = i // N_MIXERS
        rows_in = n_all if i <= last_ctx_layer else n_lat
        rows_out = n_all if i < last_ctx_layer else n_lat
        h_tail = ctx.reshape(b * cl, d) if i == 0 and rows_in == n_all else None
        h, *ffn_w = _ffn(h, mods, norm_g4, *ffn_w, i, 0, rows_in, row_of_tile, h_tail=h_tail,
                         cast_next=((w_ffn_gu, (i, 1)), (w_ffn_down, (i, 1))))
        attn = None
        if i % N_MIXERS == 0:
            lam_init = 0.8 - 0.6 * math.exp(-0.3 * i)
            q, k, v = _qkv(h, mods, norm_g4, mix_w[0], cos_t, sin_t, i, dims)
            full = i < last_ctx_layer
            o = _attention(q, k, v, da_lambda, subln_g, lam_init, j, dims, context_queries=full)
            attn = (o, mix_w[1], dims, not full)
        else:
            h = _sg_mixer(h, mods, norm_g4, mix_w[0], ln_g, ln_b, g_ws, bias_full, mix_w[1], i, j, rows_out,
                          row_of_tile)
        last = i == depth - 1
        cast_next = () if last else (((w_ffn_gu, (i + 1, 0)), (w_ffn_down, (i + 1, 0)))
                                     + mixer_weights(i + 1))
        h, *cast = _ffn(h, mods, norm_g4, *ffn_w, i, 1, rows_out, row_of_tile,
                        final_g=final_g if last else None, attn=attn, cast_next=cast_next)
        ffn_w, mix_w = cast[:2], cast[2:]
    return h.reshape(b, s, d)
```

```python
import functools
import math

import jax
import jax.numpy as jnp
from jax import lax
from jax.experimental import pallas as pl
from jax.experimental.pallas import tpu as pltpu

F32 = jnp.float32
BF16 = jnp.bfloat16

GRID_W = 64
N_MIXERS = 2
N_HEADS = 8
HEAD_DIM = 64
V_DIM = 2 * HEAD_DIM
ROPE_THETA = 10000.0
CHUNK = 128
SG_GROUPS = 8
N_MOD = 9
RMS_EPS = 1e-6
LN_EPS = 1e-5

V7X_LANES = 128
V7X_BF16_SUBLANES = 16
V7X_VMEM_BYTES = 64 * 1024 * 1024
VMEM_LIMIT_BYTES = V7X_VMEM_BYTES - 8 * 1024 * 1024

MOD_ROWS = 16
ATTN_TILE = 256
ROW_TILE = 512
HEADS_PER_STEP = 4
LOG2_E = math.log2(math.e)
MAX_UNSHIFTED_SCORE = 40.0
NORM_MARGIN = 1.05


def _params(n_axes):
    return pltpu.CompilerParams(dimension_semantics=("arbitrary",) * n_axes,
                                vmem_limit_bytes=VMEM_LIMIT_BYTES)


def _rms(x, g):
    y = x * lax.rsqrt(jnp.mean(x * x, axis=-1, keepdims=True) + RMS_EPS)
    return y * g


def _modulated_bf16(h, g, shift, scale):
    return (_rms(h, g) * (1.0 + scale) + shift).astype(BF16)


def _stacked(arr, idx, single_buffer=False):
    tail = arr.shape[len(idx):]
    index = tuple(idx) + (0,) * len(tail)
    kwargs = dict(pipeline_mode=pl.Buffered(1)) if single_buffer else {}
    return pl.BlockSpec((None,) * len(idx) + tail, lambda *_: index, **kwargs)


def _mod_spec(layer, k, d, row_of_tile):
    return pl.BlockSpec((None, None, None, 1, d), lambda t: (layer, row_of_tile(t), k, 0, 0))


def _mod_kernel(c_ref, w_ref, b_ref, o_ref):
    sc = jax.nn.silu(c_ref[...]).astype(BF16)
    o_ref[...] = jnp.dot(sc, w_ref[...].astype(BF16), preferred_element_type=F32) + b_ref[...]


def _mod_table(c_all, w_mod, b_mod):
    depth, d, nd = w_mod.shape
    tn = d
    return pl.pallas_call(
        _mod_kernel,
        grid=(depth, nd // tn),
        in_specs=[
            pl.BlockSpec((MOD_ROWS, d), lambda i, n: (0, 0)),
            pl.BlockSpec((None, d, tn), lambda i, n: (i, 0, n)),
            pl.BlockSpec((None, 1, tn), lambda i, n: (i, 0, n)),
        ],
        out_specs=pl.BlockSpec((None, MOD_ROWS, tn), lambda i, n: (i, 0, n)),
        out_shape=jax.ShapeDtypeStruct((depth, MOD_ROWS, nd), F32),
        compiler_params=_params(2),
        name="mod_table",
    )(c_all, w_mod, b_mod.reshape(depth, 1, nd))


def _ffn_kernel(*refs, n_first, attn, final, qkv, n_cast):
    refs = list(refs)
    cast_out = [refs.pop() for _ in range(n_cast)][::-1]
    qkv_out = [refs.pop() for _ in range(3 if qkv else 0)][::-1]
    o_ref = refs.pop()
    cast_in = [refs.pop() for _ in range(n_cast)][::-1]
    for src_ref, dst_ref in zip(cast_in, cast_out):
        dst_ref[...] = src_ref[...].astype(BF16)
    qkv_in = [refs.pop() for _ in range(6 if qkv else 0)][::-1]
    fg_ref = refs.pop() if final else None
    h_ref = refs.pop(0)
    h = h_ref[...]
    if n_first is not None:
        h = jnp.where(pl.program_id(0) < n_first, h, refs.pop(0)[...])
    if attn:
        attn_ref, mixer_gate_ref, wout_ref = refs.pop(0), refs.pop(0), refs.pop(0)
        heads = jnp.concatenate([attn_ref[hd] for hd in range(N_HEADS)], axis=-1)
        h = h + mixer_gate_ref[...] * jnp.dot(heads, wout_ref[...], preferred_element_type=F32)
    sh_ref, sc_ref, gt_ref, g_ref, wgu_ref, wd_ref = refs
    d_ff = wd_ref.shape[0]
    xb = _modulated_bf16(h, g_ref[...], sh_ref[...], sc_ref[...])
    a = jnp.dot(xb, wgu_ref[:, :d_ff], preferred_element_type=F32)
    u = jnp.dot(xb, wgu_ref[:, d_ff:], preferred_element_type=F32)
    act = (jax.nn.silu(a) * u).astype(BF16)
    y = jnp.dot(act, wd_ref[...], preferred_element_type=F32)
    out = h + (0.5 * gt_ref[...]) * y
    if final:
        out = _rms(out, fg_ref[...])
    o_ref[...] = out
    if qkv:
        msh_ref, msc_ref, mg_ref, win_ref, cos_ref, sin_ref = qkv_in
        xq = _modulated_bf16(out, mg_ref[...], msh_ref[...], msc_ref[...])
        _write_qkv(xq, win_ref, cos_ref[...], sin_ref[...], *qkv_out)


def _write_qkv(xb, w_ref, cos, sin, q_ref, k_ref, v_ref):
    d = xb.shape[1]
    lane = lax.broadcasted_iota(jnp.int32, cos.shape, 1)
    first_half = (lane % (HEAD_DIM // 2)) < (HEAD_DIM // 4)

    def rope(x):
        partner = jnp.where(first_half,
                            pltpu.roll(x, V7X_LANES - HEAD_DIM // 4, axis=1),
                            pltpu.roll(x, HEAD_DIM // 4, axis=1))
        return x * cos + partner * sin

    def head_major(x, ref, fn):
        for hd in range(N_HEADS):
            ref[hd] = fn(x[:, hd * V_DIM:(hd + 1) * V_DIM]).astype(BF16)

    head_major(jnp.dot(xb, w_ref[:, :d], preferred_element_type=F32), q_ref,
               lambda x: rope(x) * (LOG2_E * HEAD_DIM ** -0.5))
    head_major(jnp.dot(xb, w_ref[:, d:2 * d], preferred_element_type=F32), k_ref, rope)
    head_major(jnp.dot(xb, w_ref[:, 2 * d:], preferred_element_type=F32), v_ref, lambda x: x)


def _ffn(h, mods, norm_g, w_gu, w_down, layer, which, n_rows, row_of_tile, final_g=None, h_tail=None,
         attn=None, qkv=None, cast_next=()):
    d = h.shape[1]
    tm = ROW_TILE
    k0 = 6 * which
    final = final_g is not None
    n_first = None
    in_specs, args = [pl.BlockSpec((tm, d), lambda t: (t, 0))], [h]
    if h_tail is not None:
        n_first = h.shape[0] // tm
        in_specs = [pl.BlockSpec((tm, d), lambda t: (jnp.minimum(t, n_first - 1), 0)),
                    pl.BlockSpec((tm, d), lambda t: (jnp.maximum(t - n_first, 0), 0))]
        args = [h, h_tail]
    head_major = pl.BlockSpec((N_HEADS, tm, V_DIM), lambda t: (0, t, 0))
    if attn is not None:
        o_hm, w_out = attn
        in_specs += [head_major, _mod_spec(layer, 5, d, row_of_tile), _stacked(w_out, (), single_buffer=True)]
        args += [o_hm, mods, w_out]
    in_specs += [
        _mod_spec(layer, k0, d, row_of_tile),
        _mod_spec(layer, k0 + 1, d, row_of_tile),
        _mod_spec(layer, k0 + 2, d, row_of_tile),
        _stacked(norm_g, (layer, 2 * which)),
        _stacked(w_gu, (), single_buffer=True),
        _stacked(w_down, (), single_buffer=True),
    ]
    args += [mods, mods, mods, norm_g, w_gu, w_down]
    if final:
        in_specs.append(pl.BlockSpec((1, d), lambda t: (0, 0)))
        args.append(final_g.reshape(1, d))
    out_specs = [pl.BlockSpec((tm, d), lambda t: (t, 0))]
    out_shape = [jax.ShapeDtypeStruct((n_rows, d), F32)]
    if qkv is not None:
        w_in, cos_t, sin_t, rope_block = qkv
        in_specs += [_mod_spec(layer, 3, d, row_of_tile), _mod_spec(layer, 4, d, row_of_tile),
                     _stacked(norm_g, (layer, 1)), _stacked(w_in, (), single_buffer=True),
                     pl.BlockSpec((tm, V_DIM), lambda t: (rope_block(t), 0)),
                     pl.BlockSpec((tm, V_DIM), lambda t: (rope_block(t), 0))]
        args += [mods, mods, norm_g, w_in, cos_t, sin_t]
        out_specs += [head_major] * 3
        out_shape += [jax.ShapeDtypeStruct((N_HEADS, n_rows, V_DIM), BF16)] * 3
    n_steps = n_rows // tm
    for w32, idx in cast_next:
        rows, cols = w32.shape[len(idx):]
        n_blk = max(n for n in range(1, n_steps + 1)
                    if rows % n == 0 and (rows // n) % V7X_BF16_SUBLANES == 0)
        blk = rows // n_blk
        in_specs.append(pl.BlockSpec((None,) * len(idx) + (blk, cols),
                                     lambda t, idx=idx, n_blk=n_blk: idx + (jnp.minimum(t, n_blk - 1), 0)))
        args.append(w32)
        out_specs.append(pl.BlockSpec((blk, cols), lambda t, n_blk=n_blk: (jnp.minimum(t, n_blk - 1), 0)))
        out_shape.append(jax.ShapeDtypeStruct((rows, cols), BF16))
    return pl.pallas_call(
        functools.partial(_ffn_kernel, n_first=n_first, attn=attn is not None, final=final,
                          qkv=qkv is not None, n_cast=len(cast_next)),
        grid=(n_steps,),
        in_specs=in_specs,
        out_specs=out_specs,
        out_shape=out_shape,
        compiler_params=_params(1),
        name="ffn",
    )(*args)


def _attn_kernel(q_ref, ql_ref, qc_ref, kl_ref, kc_ref, vl_ref, vc_ref, lam_ref, sg_ref, o_ref, bounded_scr, *,
                 lam_init, ctx_tile):
    n_heads, tq, _ = q_ref.shape
    lv = lam_ref[...]
    lam = (jnp.exp(jnp.sum(lv[0:1] * lv[1:2], axis=-1, keepdims=True))
           - jnp.exp(jnp.sum(lv[2:3] * lv[3:4], axis=-1, keepdims=True)) + lam_init)
    lane = lax.broadcasted_iota(jnp.int32, (tq, V_DIM), 1)
    zero = jnp.zeros((tq, V_DIM), BF16)

    gs_row = lax.broadcasted_iota(jnp.int32, (V_DIM, V_DIM), 0)
    gs_col = lax.broadcasted_iota(jnp.int32, (V_DIM, V_DIM), 1)
    group_sum = jnp.where(gs_row // HEAD_DIM == gs_col, 1.0, 0.0).astype(BF16)

    def max_sq_norm(*xs):
        return functools.reduce(jnp.maximum, [
            jnp.max(jnp.dot(x * x, group_sum, preferred_element_type=F32), axis=0, keepdims=True) for x in xs])

    @pl.when(pl.program_id(2) == 0)
    def _():
        bound_sq = jnp.max(functools.reduce(jnp.maximum, [
            max_sq_norm(ql_ref[hd], qc_ref[hd]) * max_sq_norm(kl_ref[hd], kc_ref[hd]) for hd in range(n_heads)]))
        bounded_scr[0] = (bound_sq * NORM_MARGIN <= MAX_UNSHIFTED_SCORE ** 2).astype(jnp.int32)

    scores_bounded = bounded_scr[0] == 1
    nt = (((1,), (1,)), ((), ()))

    def attend(latent_keys, subtract_max):
        c = kc_ref.shape[1]

        def scores(hd):
            q = q_ref[hd]
            qs = jnp.concatenate([jnp.where(lane < HEAD_DIM, q, zero),
                                  jnp.where(lane >= HEAD_DIM, q, zero)], axis=0)
            s = lax.dot_general(qs, kc_ref[hd], nt, preferred_element_type=F32)
            if latent_keys:
                s = jnp.concatenate([s, lax.dot_general(qs, kl_ref[hd], nt, preferred_element_type=F32)],
                                    axis=-1)
            return s

        all_scores = [scores(hd) for hd in range(n_heads)]
        for hd in range(n_heads):
            s = all_scores[hd]
            if subtract_max:
                s = s - jnp.max(s, axis=-1, keepdims=True)
            e = jnp.exp2(s)
            inv = 1.0 / jnp.sum(e, axis=-1, keepdims=True)
            w = (e[:tq] - e[tq:] * (lam * inv[tq:] / inv[:tq])).astype(BF16)
            o = jnp.dot(w[:, :c], vc_ref[hd], preferred_element_type=F32)
            if latent_keys:
                o = o + jnp.dot(w[:, c:], vl_ref[hd], preferred_element_type=F32)
            o = _rms(o * inv[:tq], sg_ref[...]) * (1.0 - lam_init)
            o_ref[hd] = o.astype(BF16)

    def attend_guarded(latent_keys):
        pl.when(scores_bounded)(lambda: attend(latent_keys, subtract_max=False))
        pl.when(jnp.logical_not(scores_bounded))(lambda: attend(latent_keys, subtract_max=True))

    if ctx_tile:
        is_ctx = pl.program_id(2) == 0
        pl.when(is_ctx)(lambda: attend_guarded(False))
        pl.when(jnp.logical_not(is_ctx))(lambda: attend_guarded(True))
    else:
        attend_guarded(True)


def _attention(q, k, v, lam_vecs, subln_g, lam_init, j, dims, context_queries):
    b, s, c = dims
    tq = ATTN_TILE
    assert c == tq
    lat_per_b = s // tq
    ctx_blocks_from = b * s // c
    if context_queries:
        n_qt, out_rows = lat_per_b + 1, b * (s + c)
        q_block = lambda bb, jq: jnp.where(jq == 0, ctx_blocks_from + bb, bb * lat_per_b + jq - 1)
    else:
        n_qt, out_rows = lat_per_b, b * s
        q_block = lambda bb, jq: bb * lat_per_b + jq
    hb = HEADS_PER_STEP
    lat_spec = pl.BlockSpec((hb, s, V_DIM), lambda bb, hg, jq: (hg, bb, 0))
    ctx_spec = pl.BlockSpec((hb, c, V_DIM), lambda bb, hg, jq: (hg, ctx_blocks_from + bb, 0))
    tile_spec = pl.BlockSpec((hb, tq, V_DIM), lambda bb, hg, jq: (hg, q_block(bb, jq), 0))
    return pl.pallas_call(
        functools.partial(_attn_kernel, lam_init=lam_init, ctx_tile=context_queries),
        grid=(b, N_HEADS // hb, n_qt),
        in_specs=[tile_spec, lat_spec, ctx_spec, lat_spec, ctx_spec, lat_spec, ctx_spec,
                  _stacked(lam_vecs, (j,)), _stacked(subln_g, (j,))],
        out_specs=tile_spec,
        out_shape=jax.ShapeDtypeStruct((N_HEADS, out_rows, V_DIM), BF16),
        scratch_shapes=[pltpu.SMEM((1,), jnp.int32)],
        compiler_params=_params(3),
        name="attention",
    )(q, q, q, k, k, v, v, lam_vecs, subln_g)


def _sg_kernel(h_ref, sh_ref, sc_ref, gt_ref, g_ref, win_ref, lng_ref, lnb_ref, ws_ref, bs_ref, wout_ref,
               o_ref):
    tm = h_ref.shape[0]
    e = wout_ref.shape[0]
    gd = e // SG_GROUPS
    h = h_ref[...]
    xb = _modulated_bf16(h, g_ref[...], sh_ref[...], sc_ref[...])
    u = jax.nn.gelu(jnp.dot(xb, win_ref[:, :e], preferred_element_type=F32))
    v = jax.nn.gelu(jnp.dot(xb, win_ref[:, e:], preferred_element_type=F32))
    mu = jnp.mean(v, axis=-1, keepdims=True)
    vc = v - mu
    vn = vc * lax.rsqrt(jnp.mean(vc * vc, axis=-1, keepdims=True) + LN_EPS)
    vb = (vn * lng_ref[...] + lnb_ref[...]).astype(BF16)
    bias = bs_ref[...]
    rows = []
    for n in range(tm // CHUNK):
        r = slice(n * CHUNK, (n + 1) * CHUNK)
        cols = [jnp.dot(ws_ref[gi], vb[r, gi * gd:(gi + 1) * gd], preferred_element_type=F32)
                for gi in range(SG_GROUPS)]
        rows.append(jnp.concatenate(cols, axis=-1) + bias)
    mixed = jnp.concatenate(rows, axis=0)
    y = jnp.dot((u * mixed).astype(BF16), wout_ref[...], preferred_element_type=F32)
    o_ref[...] = h + gt_ref[...] * y


def _sg_mixer(h, mods, norm_g, w_in, ln_g, ln_b, w_s, bias_full, w_out, layer, j, n_rows, row_of_tile):
    d = h.shape[1]
    tm = ROW_TILE
    return pl.pallas_call(
        _sg_kernel,
        grid=(n_rows // tm,),
        in_specs=[
            pl.BlockSpec((tm, d), lambda t: (t, 0)),
            _mod_spec(layer, 3, d, row_of_tile),
            _mod_spec(layer, 4, d, row_of_tile),
            _mod_spec(layer, 5, d, row_of_tile),
            _stacked(norm_g, (layer, 1)),
            _stacked(w_in, (), single_buffer=True),
            _stacked(ln_g, (j,)),
            _stacked(ln_b, (j,)),
            _stacked(w_s, (j,)),
            _stacked(bias_full, (j,)),
            _stacked(w_out, (), single_buffer=True),
        ],
        out_specs=pl.BlockSpec((tm, d), lambda t: (t, 0)),
        out_shape=jax.ShapeDtypeStruct((n_rows, d), F32),
        compiler_params=_params(1),
        name="sg_mixer",
    )(h, mods, mods, mods, norm_g, w_in, ln_g, ln_b, w_s, bias_full, w_out)


def _rope_tables(s, tm):
    rows_n = s // GRID_W
    row = jnp.repeat(jnp.arange(rows_n), GRID_W)
    col = jnp.tile(jnp.arange(GRID_W), rows_n)
    n_freq = HEAD_DIM // 4
    inv = ROPE_THETA ** (-jnp.arange(n_freq, dtype=F32) / n_freq)
    pos = jnp.stack([row, col], axis=-1).astype(F32)
    ang = pos[:, :, None] * inv
    cos, sin = jnp.cos(ang), jnp.sin(ang)
    cos_l = jnp.broadcast_to(cos[:, None, :, None, :], (s, 2, 2, 2, n_freq)).reshape(s, V_DIM)
    sign = jnp.array([-1.0, 1.0], F32)[None, None, None, :, None]
    sin_l = jnp.broadcast_to(sin[:, None, :, None, :] * sign, (s, 2, 2, 2, n_freq)).reshape(s, V_DIM)
    cos_t = jnp.concatenate([cos_l, jnp.ones((tm, V_DIM), F32)], axis=0)
    sin_t = jnp.concatenate([sin_l, jnp.zeros((tm, V_DIM), F32)], axis=0)
    return cos_t, sin_t


def kernel(x, c, ctx, c_ctx, w_mod, b_mod, norm_g, w_ffn_gu, w_ffn_down, da_w_in, da_w_out, da_lambda,
           da_subln_g, sg_w_in, sg_ln_g, sg_ln_b, sg_w_s, sg_b_s, sg_w_out, final_g):
    b, s, d = x.shape
    cl = ctx.shape[1]
    depth = w_mod.shape[0]
    e = sg_w_out.shape[1]
    assert cl == ATTN_TILE and s % ROW_TILE == 0 and s % GRID_W == 0 and b < MOD_ROWS
    assert (b * cl) % ROW_TILE == 0 and ROW_TILE % CHUNK == 0 and cl % CHUNK == 0
    dims = (b, s, cl)
    n_lat, n_all = b * s, b * (s + cl)
    tiles_per_b = s // ROW_TILE

    def row_of_tile(t):
        return jnp.minimum(t // tiles_per_b, b)

    def rope_block(t):
        return jnp.where(t < n_lat // ROW_TILE, t % tiles_per_b, tiles_per_b)

    c_all = jnp.zeros((MOD_ROWS, d), F32).at[:b].set(c).at[b].set(c_ctx)
    mods = _mod_table(c_all, w_mod, b_mod).reshape(depth, MOD_ROWS, N_MOD, 1, d)
    cos_t, sin_t = _rope_tables(s, ROW_TILE)

    norm_g4 = norm_g.reshape(depth, 3, 1, d)
    ffn_w = [w_ffn_gu[0, 0].astype(BF16), w_ffn_down[0, 0].astype(BF16)]

    def mixer_weights(i):
        w_in, w_out = (da_w_in, da_w_out) if i % N_MIXERS == 0 else (sg_w_in, sg_w_out)
        return (w_in, (i // N_MIXERS,)), (w_out, (i // N_MIXERS,))

    mix_w = [w[idx].astype(BF16) for w, idx in mixer_weights(0)]
    g_ws = sg_w_s.astype(BF16)
    subln_g = da_subln_g.reshape(-1, 1, V_DIM)
    ln_g, ln_b = sg_ln_g.reshape(-1, 1, e), sg_ln_b.reshape(-1, 1, e)
    bias_full = jnp.repeat(jnp.swapaxes(sg_b_s, 1, 2), e // SG_GROUPS, axis=2)

    last_ctx_layer = max(i for i in range(depth) if i % N_MIXERS == 0)
    h = x.reshape(n_lat, d)
    for i in range(depth):
        j = i // N_MIXERS
        attention_layer = i % N_MIXERS == 0
        rows_in = n_all if i <= last_ctx_layer else n_lat
        rows_out = n_all if i < last_ctx_layer else n_lat
        h_tail = ctx.reshape(b * cl, d) if i == 0 and rows_in == n_all else None
        assert not attention_layer or rows_in == n_all
        h, *rest = _ffn(h, mods, norm_g4, *ffn_w, i, 0, rows_in, row_of_tile, h_tail=h_tail,
                        qkv=(mix_w[0], cos_t, sin_t, rope_block) if attention_layer else None,
                        cast_next=((w_ffn_gu, (i, 1)), (w_ffn_down, (i, 1))))
        ffn_w = rest[-2:]
        attn = None
        if attention_layer:
            q, k, v = rest[:3]
            lam_init = 0.8 - 0.6 * math.exp(-0.3 * i)
            o = _attention(q, k, v, da_lambda, subln_g, lam_init, j, dims,
                           context_queries=i < last_ctx_layer)
            attn = (o, mix_w[1])
        else:
            h = _sg_mixer(h, mods, norm_g4, mix_w[0], ln_g, ln_b, g_ws, bias_full, mix_w[1], i, j, rows_out,
                          row_of_tile)
        last = i == depth - 1
        cast_next = () if last else (((w_ffn_gu, (i + 1, 0)), (w_ffn_down, (i + 1, 0)))
                                     + mixer_weights(i + 1))
        h, *cast = _ffn(h, mods, norm_g4, *ffn_w, i, 1, rows_out, row_of_tile,
                        final_g=final_g if last else None, attn=attn, cast_next=cast_next)
        ffn_w, mix_w = cast[:2], cast[2:]
    return h.reshape(b, s, d)
```

```python
import functools
import math

import jax
import jax.numpy as jnp
from jax import lax
from jax.experimental import pallas as pl
from jax.experimental.pallas import tpu as pltpu

F32 = jnp.float32
BF16 = jnp.bfloat16

GRID_W = 64
N_MIXERS = 2
N_HEADS = 8
HEAD_DIM = 64
V_DIM = 2 * HEAD_DIM
ROPE_THETA = 10000.0
CHUNK = 128
SG_GROUPS = 8
N_MOD = 9
RMS_EPS = 1e-6
LN_EPS = 1e-5

V7X_LANES = 128
V7X_BF16_SUBLANES = 16
V7X_VMEM_BYTES = 64 * 1024 * 1024
VMEM_LIMIT_BYTES = V7X_VMEM_BYTES - 8 * 1024 * 1024

MOD_ROWS = 16
ATTN_TILE = 256
ROW_TILE = 512
HEADS_PER_STEP = 4
LOG2_E = math.log2(math.e)
MAX_UNSHIFTED_SCORE = 40.0
NORM_MARGIN = 1.05


def _params(n_axes):
    return pltpu.CompilerParams(dimension_semantics=("arbitrary",) * n_axes,
                                vmem_limit_bytes=VMEM_LIMIT_BYTES)


def _rms(x, g):
    y = x * lax.rsqrt(jnp.mean(x * x, axis=-1, keepdims=True) + RMS_EPS)
    return y * g


def _modulated_bf16(h, g, shift, scale):
    r = lax.rsqrt(jnp.mean(h * h, axis=-1, keepdims=True) + RMS_EPS)
    return ((h * r) * (g * (1.0 + scale)) + shift).astype(BF16)


def _gelu_tanh(x):
    c1 = -2.0 * LOG2_E * math.sqrt(2.0 / math.pi)
    return x / (1.0 + jnp.exp2(x * (c1 + (0.044715 * c1) * (x * x))))


def _stacked(arr, idx, single_buffer=False):
    tail = arr.shape[len(idx):]
    index = tuple(idx) + (0,) * len(tail)
    kwargs = dict(pipeline_mode=pl.Buffered(1)) if single_buffer else {}
    return pl.BlockSpec((None,) * len(idx) + tail, lambda *_: index, **kwargs)


def _mod_spec(layer, k, d, row_of_tile):
    return pl.BlockSpec((None, None, None, 1, d), lambda t: (layer, row_of_tile(t), k, 0, 0))


def _mod_kernel(c_ref, w_ref, b_ref, o_ref):
    sc = jax.nn.silu(c_ref[...]).astype(BF16)
    o_ref[...] = jnp.dot(sc, w_ref[...].astype(BF16), preferred_element_type=F32) + b_ref[...]


def _mod_table(c_all, w_mod, b_mod):
    depth, d, nd = w_mod.shape
    tn = d
    return pl.pallas_call(
        _mod_kernel,
        grid=(depth, nd // tn),
        in_specs=[
            pl.BlockSpec((MOD_ROWS, d), lambda i, n: (0, 0)),
            pl.BlockSpec((None, d, tn), lambda i, n: (i, 0, n)),
            pl.BlockSpec((None, 1, tn), lambda i, n: (i, 0, n)),
        ],
        out_specs=pl.BlockSpec((None, MOD_ROWS, tn), lambda i, n: (i, 0, n)),
        out_shape=jax.ShapeDtypeStruct((depth, MOD_ROWS, nd), F32),
        compiler_params=_params(2),
        name="mod_table",
    )(c_all, w_mod, b_mod.reshape(depth, 1, nd))


def _ffn_kernel(*refs, n_first, attn, final, qkv, n_cast):
    refs = list(refs)
    cast_out = [refs.pop() for _ in range(n_cast)][::-1]
    qkv_out = [refs.pop() for _ in range(3 if qkv else 0)][::-1]
    o_ref = refs.pop()
    cast_in = [refs.pop() for _ in range(n_cast)][::-1]
    for src_ref, dst_ref in zip(cast_in, cast_out):
        dst_ref[...] = src_ref[...].astype(BF16)
    qkv_in = [refs.pop() for _ in range(6 if qkv else 0)][::-1]
    fg_ref = refs.pop() if final else None
    h_ref = refs.pop(0)
    h = h_ref[...]
    if n_first is not None:
        h = jnp.where(pl.program_id(0) < n_first, h, refs.pop(0)[...])
    if attn:
        attn_ref, mixer_gate_ref, wout_ref = refs.pop(0), refs.pop(0), refs.pop(0)
        heads = jnp.concatenate([attn_ref[hd] for hd in range(N_HEADS)], axis=-1)
        h = h + mixer_gate_ref[...] * jnp.dot(heads, wout_ref[...], preferred_element_type=F32)
    sh_ref, sc_ref, gt_ref, g_ref, wgu_ref, wd_ref = refs
    d_ff = wd_ref.shape[0]
    xb = _modulated_bf16(h, g_ref[...], sh_ref[...], sc_ref[...])
    half = h.shape[0] // 2
    ys = []
    for part in range(2):
        xp = xb[part * half:(part + 1) * half]
        a = jnp.dot(xp, wgu_ref[:, :d_ff], preferred_element_type=F32)
        u = jnp.dot(xp, wgu_ref[:, d_ff:], preferred_element_type=F32)
        act = (jax.nn.silu(a) * u).astype(BF16)
        ys.append(jnp.dot(act, wd_ref[...], preferred_element_type=F32))
    y = jnp.concatenate(ys, axis=0)
    out = h + (0.5 * gt_ref[...]) * y
    if final:
        out = _rms(out, fg_ref[...])
    o_ref[...] = out
    if qkv:
        msh_ref, msc_ref, mg_ref, win_ref, cos_ref, sin_ref = qkv_in
        xq = _modulated_bf16(out, mg_ref[...], msh_ref[...], msc_ref[...])
        _write_qkv(xq, win_ref, cos_ref[...], sin_ref[...], *qkv_out)


def _write_qkv(xb, w_ref, cos, sin, q_ref, k_ref, v_ref):
    d = xb.shape[1]
    lane = lax.broadcasted_iota(jnp.int32, cos.shape, 1)
    first_half = (lane % (HEAD_DIM // 2)) < (HEAD_DIM // 4)

    def rope(x):
        partner = jnp.where(first_half,
                            pltpu.roll(x, V7X_LANES - HEAD_DIM // 4, axis=1),
                            pltpu.roll(x, HEAD_DIM // 4, axis=1))
        return x * cos + partner * sin

    def head_major(x, ref, fn):
        for hd in range(N_HEADS):
            ref[hd] = fn(x[:, hd * V_DIM:(hd + 1) * V_DIM]).astype(BF16)

    head_major(jnp.dot(xb, w_ref[:, :d], preferred_element_type=F32), q_ref,
               lambda x: rope(x) * (LOG2_E * HEAD_DIM ** -0.5))
    head_major(jnp.dot(xb, w_ref[:, d:2 * d], preferred_element_type=F32), k_ref, rope)
    head_major(jnp.dot(xb, w_ref[:, 2 * d:], preferred_element_type=F32), v_ref, lambda x: x)


def _ffn(h, mods, norm_g, w_gu, w_down, layer, which, n_rows, row_of_tile, final_g=None, h_tail=None,
         attn=None, qkv=None, cast_next=()):
    d = h.shape[1]
    tm = ROW_TILE
    k0 = 6 * which
    final = final_g is not None
    n_first = None
    in_specs, args = [pl.BlockSpec((tm, d), lambda t: (t, 0))], [h]
    if h_tail is not None:
        n_first = h.shape[0] // tm
        in_specs = [pl.BlockSpec((tm, d), lambda t: (jnp.minimum(t, n_first - 1), 0)),
                    pl.BlockSpec((tm, d), lambda t: (jnp.maximum(t - n_first, 0), 0))]
        args = [h, h_tail]
    head_major = pl.BlockSpec((N_HEADS, tm, V_DIM), lambda t: (0, t, 0))
    if attn is not None:
        o_hm, w_out = attn
        in_specs += [head_major, _mod_spec(layer, 5, d, row_of_tile), _stacked(w_out, (), single_buffer=True)]
        args += [o_hm, mods, w_out]
    in_specs += [
        _mod_spec(layer, k0, d, row_of_tile),
        _mod_spec(layer, k0 + 1, d, row_of_tile),
        _mod_spec(layer, k0 + 2, d, row_of_tile),
        _stacked(norm_g, (layer, 2 * which)),
        _stacked(w_gu, (), single_buffer=True),
        _stacked(w_down, (), single_buffer=True),
    ]
    args += [mods, mods, mods, norm_g, w_gu, w_down]
    if final:
        in_specs.append(pl.BlockSpec((1, d), lambda t: (0, 0)))
        args.append(final_g.reshape(1, d))
    out_specs = [pl.BlockSpec((tm, d), lambda t: (t, 0))]
    out_shape = [jax.ShapeDtypeStruct((n_rows, d), F32)]
    if qkv is not None:
        w_in, cos_t, sin_t, rope_block = qkv
        in_specs += [_mod_spec(layer, 3, d, row_of_tile), _mod_spec(layer, 4, d, row_of_tile),
                     _stacked(norm_g, (layer, 1)), _stacked(w_in, (), single_buffer=True),
                     pl.BlockSpec((tm, V_DIM), lambda t: (rope_block(t), 0)),
                     pl.BlockSpec((tm, V_DIM), lambda t: (rope_block(t), 0))]
        args += [mods, mods, norm_g, w_in, cos_t, sin_t]
        out_specs += [head_major] * 3
        out_shape += [jax.ShapeDtypeStruct((N_HEADS, n_rows, V_DIM), BF16)] * 3
    n_steps = n_rows // tm
    for w32, idx in cast_next:
        rows, cols = w32.shape[len(idx):]
        n_blk = max(n for n in range(1, n_steps + 1)
                    if rows % n == 0 and (rows // n) % V7X_BF16_SUBLANES == 0)
        blk = rows // n_blk
        in_specs.append(pl.BlockSpec((None,) * len(idx) + (blk, cols),
                                     lambda t, idx=idx, n_blk=n_blk: idx + (jnp.minimum(t, n_blk - 1), 0)))
        args.append(w32)
        out_specs.append(pl.BlockSpec((blk, cols), lambda t, n_blk=n_blk: (jnp.minimum(t, n_blk - 1), 0)))
        out_shape.append(jax.ShapeDtypeStruct((rows, cols), BF16))
    return pl.pallas_call(
        functools.partial(_ffn_kernel, n_first=n_first, attn=attn is not None, final=final,
                          qkv=qkv is not None, n_cast=len(cast_next)),
        grid=(n_steps,),
        in_specs=in_specs,
        out_specs=out_specs,
        out_shape=out_shape,
        compiler_params=_params(1),
        name="ffn",
    )(*args)


def _attn_kernel(q_ref, ql_ref, qc_ref, kl_ref, kc_ref, vl_ref, vc_ref, lam_ref, sg_ref, o_ref, bounded_scr, *,
                 lam_init, ctx_tile):
    n_heads, tq, _ = q_ref.shape
    lv = lam_ref[...]
    lam = (jnp.exp(jnp.sum(lv[0:1] * lv[1:2], axis=-1, keepdims=True))
           - jnp.exp(jnp.sum(lv[2:3] * lv[3:4], axis=-1, keepdims=True)) + lam_init)
    lane = lax.broadcasted_iota(jnp.int32, (tq, V_DIM), 1)
    zero = jnp.zeros((tq, V_DIM), BF16)

    gs_row = lax.broadcasted_iota(jnp.int32, (V_DIM, V_DIM), 0)
    gs_col = lax.broadcasted_iota(jnp.int32, (V_DIM, V_DIM), 1)
    group_sum = jnp.where(gs_row // HEAD_DIM == gs_col, 1.0, 0.0).astype(BF16)

    def max_sq_norm(*xs):
        return functools.reduce(jnp.maximum, [
            jnp.max(jnp.dot(x * x, group_sum, preferred_element_type=F32), axis=0, keepdims=True) for x in xs])

    @pl.when(pl.program_id(2) == 0)
    def _():
        bound_sq = jnp.max(functools.reduce(jnp.maximum, [
            max_sq_norm(ql_ref[hd], qc_ref[hd]) * max_sq_norm(kl_ref[hd], kc_ref[hd]) for hd in range(n_heads)]))
        bounded_scr[0] = (bound_sq * NORM_MARGIN <= MAX_UNSHIFTED_SCORE ** 2).astype(jnp.int32)

    scores_bounded = bounded_scr[0] == 1
    nt = (((1,), (1,)), ((), ()))

    def attend(latent_keys, subtract_max):
        c = kc_ref.shape[1]

        def scores(hd):
            q = q_ref[hd]
            qs = jnp.concatenate([jnp.where(lane < HEAD_DIM, q, zero),
                                  jnp.where(lane >= HEAD_DIM, q, zero)], axis=0)
            s = lax.dot_general(qs, kc_ref[hd], nt, preferred_element_type=F32)
            if latent_keys:
                s = jnp.concatenate([s, lax.dot_general(qs, kl_ref[hd], nt, preferred_element_type=F32)],
                                    axis=-1)
            return s

        all_scores = [scores(hd) for hd in range(n_heads)]
        for hd in range(n_heads):
            s = all_scores[hd]
            if subtract_max:
                s = s - jnp.max(s, axis=-1, keepdims=True)
            e = jnp.exp2(s)
            inv = 1.0 / jnp.sum(e, axis=-1, keepdims=True)
            w = (e[:tq] - e[tq:] * (lam * inv[tq:] / inv[:tq])).astype(BF16)
            o = jnp.dot(w[:, :c], vc_ref[hd], preferred_element_type=F32)
            if latent_keys:
                o = o + jnp.dot(w[:, c:], vl_ref[hd], preferred_element_type=F32)
            o = _rms(o * inv[:tq], sg_ref[...]) * (1.0 - lam_init)
            o_ref[hd] = o.astype(BF16)

    def attend_guarded(latent_keys):
        pl.when(scores_bounded)(lambda: attend(latent_keys, subtract_max=False))
        pl.when(jnp.logical_not(scores_bounded))(lambda: attend(latent_keys, subtract_max=True))

    if ctx_tile:
        is_ctx = pl.program_id(2) == 0
        pl.when(is_ctx)(lambda: attend_guarded(False))
        pl.when(jnp.logical_not(is_ctx))(lambda: attend_guarded(True))
    else:
        attend_guarded(True)


def _attention(q, k, v, lam_vecs, subln_g, lam_init, j, dims, context_queries):
    b, s, c = dims
    tq = ATTN_TILE
    assert c == tq
    lat_per_b = s // tq
    ctx_blocks_from = b * s // c
    if context_queries:
        n_qt, out_rows = lat_per_b + 1, b * (s + c)
        q_block = lambda bb, jq: jnp.where(jq == 0, ctx_blocks_from + bb, bb * lat_per_b + jq - 1)
    else:
        n_qt, out_rows = lat_per_b, b * s
        q_block = lambda bb, jq: bb * lat_per_b + jq
    hb = HEADS_PER_STEP
    lat_spec = pl.BlockSpec((hb, s, V_DIM), lambda bb, hg, jq: (hg, bb, 0))
    ctx_spec = pl.BlockSpec((hb, c, V_DIM), lambda bb, hg, jq: (hg, ctx_blocks_from + bb, 0))
    tile_spec = pl.BlockSpec((hb, tq, V_DIM), lambda bb, hg, jq: (hg, q_block(bb, jq), 0))
    return pl.pallas_call(
        functools.partial(_attn_kernel, lam_init=lam_init, ctx_tile=context_queries),
        grid=(b, N_HEADS // hb, n_qt),
        in_specs=[tile_spec, lat_spec, ctx_spec, lat_spec, ctx_spec, lat_spec, ctx_spec,
                  _stacked(lam_vecs, (j,)), _stacked(subln_g, (j,))],
        out_specs=tile_spec,
        out_shape=jax.ShapeDtypeStruct((N_HEADS, out_rows, V_DIM), BF16),
        scratch_shapes=[pltpu.SMEM((1,), jnp.int32)],
        compiler_params=_params(3),
        name="attention",
    )(q, q, q, k, k, v, v, lam_vecs, subln_g)


def _sg_kernel(h_ref, sh_ref, sc_ref, gt_ref, g_ref, win_ref, lng_ref, lnb_ref, ws_ref, bs_ref, wout_ref,
               o_ref):
    tm = h_ref.shape[0]
    e = wout_ref.shape[0]
    gd = e // SG_GROUPS
    h = h_ref[...]
    xb = _modulated_bf16(h, g_ref[...], sh_ref[...], sc_ref[...])
    u = _gelu_tanh(jnp.dot(xb, win_ref[:, :e], preferred_element_type=F32))
    v = _gelu_tanh(jnp.dot(xb, win_ref[:, e:], preferred_element_type=F32))
    mu = jnp.mean(v, axis=-1, keepdims=True)
    vc = v - mu
    vn = vc * lax.rsqrt(jnp.mean(vc * vc, axis=-1, keepdims=True) + LN_EPS)
    vb = (vn * lng_ref[...] + lnb_ref[...]).astype(BF16)
    bias = bs_ref[...]
    rows = []
    for n in range(tm // CHUNK):
        r = slice(n * CHUNK, (n + 1) * CHUNK)
        cols = [jnp.dot(ws_ref[gi], vb[r, gi * gd:(gi + 1) * gd], preferred_element_type=F32)
                for gi in range(SG_GROUPS)]
        rows.append(jnp.concatenate(cols, axis=-1) + bias)
    mixed = jnp.concatenate(rows, axis=0)
    y = jnp.dot((u * mixed).astype(BF16), wout_ref[...], preferred_element_type=F32)
    o_ref[...] = h + gt_ref[...] * y


def _sg_mixer(h, mods, norm_g, w_in, ln_g, ln_b, w_s, bias_full, w_out, layer, j, n_rows, row_of_tile):
    d = h.shape[1]
    tm = ROW_TILE
    return pl.pallas_call(
        _sg_kernel,
        grid=(n_rows // tm,),
        in_specs=[
            pl.BlockSpec((tm, d), lambda t: (t, 0)),
            _mod_spec(layer, 3, d, row_of_tile),
            _mod_spec(layer, 4, d, row_of_tile),
            _mod_spec(layer, 5, d, row_of_tile),
            _stacked(norm_g, (layer, 1)),
            _stacked(w_in, (), single_buffer=True),
            _stacked(ln_g, (j,)),
            _stacked(ln_b, (j,)),
            _stacked(w_s, (j,)),
            _stacked(bias_full, (j,)),
            _stacked(w_out, (), single_buffer=True),
        ],
        out_specs=pl.BlockSpec((tm, d), lambda t: (t, 0)),
        out_shape=jax.ShapeDtypeStruct((n_rows, d), F32),
        compiler_params=_params(1),
        name="sg_mixer",
    )(h, mods, mods, mods, norm_g, w_in, ln_g, ln_b, w_s, bias_full, w_out)


def _rope_tables(s, tm):
    rows_n = s // GRID_W
    row = jnp.repeat(jnp.arange(rows_n), GRID_W)
    col = jnp.tile(jnp.arange(GRID_W), rows_n)
    n_freq = HEAD_DIM // 4
    inv = ROPE_THETA ** (-jnp.arange(n_freq, dtype=F32) / n_freq)
    pos = jnp.stack([row, col], axis=-1).astype(F32)
    ang = pos[:, :, None] * inv
    cos, sin = jnp.cos(ang), jnp.sin(ang)
    cos_l = jnp.broadcast_to(cos[:, None, :, None, :], (s, 2, 2, 2, n_freq)).reshape(s, V_DIM)
    sign = jnp.array([-1.0, 1.0], F32)[None, None, None, :, None]
    sin_l = jnp.broadcast_to(sin[:, None, :, None, :] * sign, (s, 2, 2, 2, n_freq)).reshape(s, V_DIM)
    cos_t = jnp.concatenate([cos_l, jnp.ones((tm, V_DIM), F32)], axis=0)
    sin_t = jnp.concatenate([sin_l, jnp.zeros((tm, V_DIM), F32)], axis=0)
    return cos_t, sin_t


def kernel(x, c, ctx, c_ctx, w_mod, b_mod, norm_g, w_ffn_gu, w_ffn_down, da_w_in, da_w_out, da_lambda,
           da_subln_g, sg_w_in, sg_ln_g, sg_ln_b, sg_w_s, sg_b_s, sg_w_out, final_g):
    b, s, d = x.shape
    cl = ctx.shape[1]
    depth = w_mod.shape[0]
    e = sg_w_out.shape[1]
    assert cl == ATTN_TILE and s % ROW_TILE == 0 and s % GRID_W == 0 and b < MOD_ROWS
    assert (b * cl) % ROW_TILE == 0 and ROW_TILE % CHUNK == 0 and cl % CHUNK == 0
    dims = (b, s, cl)
    n_lat, n_all = b * s, b * (s + cl)
    tiles_per_b = s // ROW_TILE

    def row_of_tile(t):
        return jnp.minimum(t // tiles_per_b, b)

    def rope_block(t):
        return jnp.where(t < n_lat // ROW_TILE, t % tiles_per_b, tiles_per_b)

    c_all = jnp.zeros((MOD_ROWS, d), F32).at[:b].set(c).at[b].set(c_ctx)
    mods = _mod_table(c_all, w_mod, b_mod).reshape(depth, MOD_ROWS, N_MOD, 1, d)
    cos_t, sin_t = _rope_tables(s, ROW_TILE)

    norm_g4 = norm_g.reshape(depth, 3, 1, d)
    ffn_w = [w_ffn_gu[0, 0].astype(BF16), w_ffn_down[0, 0].astype(BF16)]

    def mixer_weights(i):
        w_in, w_out = (da_w_in, da_w_out) if i % N_MIXERS == 0 else (sg_w_in, sg_w_out)
        return (w_in, (i // N_MIXERS,)), (w_out, (i // N_MIXERS,))

    mix_w = [w[idx].astype(BF16) for w, idx in mixer_weights(0)]
    g_ws = sg_w_s.astype(BF16)
    subln_g = da_subln_g.reshape(-1, 1, V_DIM)
    ln_g, ln_b = sg_ln_g.reshape(-1, 1, e), sg_ln_b.reshape(-1, 1, e)
    bias_full = jnp.repeat(jnp.swapaxes(sg_b_s, 1, 2), e // SG_GROUPS, axis=2)

    last_ctx_layer = max(i for i in range(depth) if i % N_MIXERS == 0)
    h = x.reshape(n_lat, d)
    for i in range(depth):
        j = i // N_MIXERS
        attention_layer = i % N_MIXERS == 0
        rows_in = n_all if i <= last_ctx_layer else n_lat
        rows_out = n_all if i < last_ctx_layer else n_lat
        h_tail = ctx.reshape(b * cl, d) if i == 0 and rows_in == n_all else None
        assert not attention_layer or rows_in == n_all
        h, *rest = _ffn(h, mods, norm_g4, *ffn_w, i, 0, rows_in, row_of_tile, h_tail=h_tail,
                        qkv=(mix_w[0], cos_t, sin_t, rope_block) if attention_layer else None,
                        cast_next=((w_ffn_gu, (i, 1)), (w_ffn_down, (i, 1))))
        ffn_w = rest[-2:]
        attn = None
        if attention_layer:
            q, k, v = rest[:3]
            lam_init = 0.8 - 0.6 * math.exp(-0.3 * i)
            o = _attention(q, k, v, da_lambda, subln_g, lam_init, j, dims,
                           context_queries=i < last_ctx_layer)
            attn = (o, mix_w[1])
        else:
            h = _sg_mixer(h, mods, norm_g4, mix_w[0], ln_g, ln_b, g_ws, bias_full, mix_w[1], i, j, rows_out,
                          row_of_tile)
        last = i == depth - 1
        cast_next = () if last else (((w_ffn_gu, (i + 1, 0)), (w_ffn_down, (i + 1, 0)))
                                     + mixer_weights(i + 1))
        h, *cast = _ffn(h, mods, norm_g4, *ffn_w, i, 1, rows_out, row_of_tile,
                        final_g=final_g if last else None, attn=attn, cast_next=cast_next)
        ffn_w, mix_w = cast[:2], cast[2:]
    return h.reshape(b, s, d)
```

```python
import functools
import math

import jax
import jax.numpy as jnp
from jax import lax
from jax.experimental import pallas as pl
from jax.experimental.pallas import tpu as pltpu

F32 = jnp.float32
BF16 = jnp.bfloat16

GRID_W = 64
N_MIXERS = 2
N_HEADS = 8
HEAD_DIM = 64
V_DIM = 2 * HEAD_DIM
ROPE_THETA = 10000.0
CHUNK = 128
SG_GROUPS = 8
N_MOD = 9
RMS_EPS = 1e-6
LN_EPS = 1e-5

V7X_LANES = 128
V7X_BF16_SUBLANES = 16
V7X_VMEM_BYTES = 64 * 1024 * 1024
VMEM_LIMIT_BYTES = V7X_VMEM_BYTES - 8 * 1024 * 1024

MOD_ROWS = 16
ATTN_TILE = 256
ROW_TILE = 512
FFN_ROW_CHAINS = 4
HEADS_PER_STEP = 4
LOG2_E = math.log2(math.e)
MAX_UNSHIFTED_SCORE = 40.0
NORM_MARGIN = 1.05


def _params(n_axes):
    return pltpu.CompilerParams(dimension_semantics=("arbitrary",) * n_axes,
                                vmem_limit_bytes=VMEM_LIMIT_BYTES)


def _rms(x, g):
    y = x * lax.rsqrt(jnp.mean(x * x, axis=-1, keepdims=True) + RMS_EPS)
    return y * g


def _modulated_bf16(h, g, shift, scale):
    r = lax.rsqrt(jnp.mean(h * h, axis=-1, keepdims=True) + RMS_EPS)
    return ((h * r) * (g * (1.0 + scale)) + shift).astype(BF16)


def _gelu_tanh(x):
    c1 = -2.0 * LOG2_E * math.sqrt(2.0 / math.pi)
    return x / (1.0 + jnp.exp2(x * (c1 + (0.044715 * c1) * (x * x))))


def _stacked(arr, idx, single_buffer=False):
    tail = arr.shape[len(idx):]
    index = tuple(idx) + (0,) * len(tail)
    kwargs = dict(pipeline_mode=pl.Buffered(1)) if single_buffer else {}
    return pl.BlockSpec((None,) * len(idx) + tail, lambda *_: index, **kwargs)


def _mod_spec(layer, k, d, row_of_tile):
    return pl.BlockSpec((None, None, None, 1, d), lambda t: (layer, row_of_tile(t), k, 0, 0))


def _mod_kernel(c_ref, w_ref, b_ref, o_ref):
    sc = jax.nn.silu(c_ref[...]).astype(BF16)
    o_ref[...] = jnp.dot(sc, w_ref[...].astype(BF16), preferred_element_type=F32) + b_ref[...]


def _mod_table(c_all, w_mod, b_mod):
    depth, d, nd = w_mod.shape
    tn = d
    return pl.pallas_call(
        _mod_kernel,
        grid=(depth, nd // tn),
        in_specs=[
            pl.BlockSpec((MOD_ROWS, d), lambda i, n: (0, 0)),
            pl.BlockSpec((None, d, tn), lambda i, n: (i, 0, n)),
            pl.BlockSpec((None, 1, tn), lambda i, n: (i, 0, n)),
        ],
        out_specs=pl.BlockSpec((None, MOD_ROWS, tn), lambda i, n: (i, 0, n)),
        out_shape=jax.ShapeDtypeStruct((depth, MOD_ROWS, nd), F32),
        compiler_params=_params(2),
        name="mod_table",
    )(c_all, w_mod, b_mod.reshape(depth, 1, nd))


def _ffn_kernel(*refs, n_first, attn, final, qkv, n_cast):
    refs = list(refs)
    cast_out = [refs.pop() for _ in range(n_cast)][::-1]
    qkv_out = [refs.pop() for _ in range(3 if qkv else 0)][::-1]
    o_ref = refs.pop()
    cast_in = [refs.pop() for _ in range(n_cast)][::-1]
    for src_ref, dst_ref in zip(cast_in, cast_out):
        dst_ref[...] = src_ref[...].astype(BF16)
    qkv_in = [refs.pop() for _ in range(6 if qkv else 0)][::-1]
    fg_ref = refs.pop() if final else None
    h_ref = refs.pop(0)
    h = h_ref[...]
    if n_first is not None:
        h = jnp.where(pl.program_id(0) < n_first, h, refs.pop(0)[...])
    if attn:
        attn_ref, mixer_gate_ref, wout_ref = refs.pop(0), refs.pop(0), refs.pop(0)
        heads = jnp.concatenate([attn_ref[hd] for hd in range(N_HEADS)], axis=-1)
        h = h + mixer_gate_ref[...] * jnp.dot(heads, wout_ref[...], preferred_element_type=F32)
    sh_ref, sc_ref, gt_ref, g_ref, wgu_ref, wd_ref = refs
    d_ff = wd_ref.shape[0]
    xb = _modulated_bf16(h, g_ref[...], sh_ref[...], sc_ref[...])
    rows = h.shape[0] // FFN_ROW_CHAINS
    ys = []
    for part in range(FFN_ROW_CHAINS):
        xp = xb[part * rows:(part + 1) * rows]
        a = jnp.dot(xp, wgu_ref[:, :d_ff], preferred_element_type=F32)
        u = jnp.dot(xp, wgu_ref[:, d_ff:], preferred_element_type=F32)
        act = (jax.nn.silu(a) * u).astype(BF16)
        ys.append(jnp.dot(act, wd_ref[...], preferred_element_type=F32))
    y = jnp.concatenate(ys, axis=0)
    out = h + (0.5 * gt_ref[...]) * y
    if final:
        out = _rms(out, fg_ref[...])
    o_ref[...] = out
    if qkv:
        msh_ref, msc_ref, mg_ref, win_ref, cos_ref, sin_ref = qkv_in
        xq = _modulated_bf16(out, mg_ref[...], msh_ref[...], msc_ref[...])
        _write_qkv(xq, win_ref, cos_ref[...], sin_ref[...], *qkv_out)


def _write_qkv(xb, w_ref, cos, sin, q_ref, k_ref, v_ref):
    d = xb.shape[1]
    lane = lax.broadcasted_iota(jnp.int32, cos.shape, 1)
    first_half = (lane % (HEAD_DIM // 2)) < (HEAD_DIM // 4)

    def rope(x):
        partner = jnp.where(first_half,
                            pltpu.roll(x, V7X_LANES - HEAD_DIM // 4, axis=1),
                            pltpu.roll(x, HEAD_DIM // 4, axis=1))
        return x * cos + partner * sin

    def head_major(x, ref, fn):
        for hd in range(N_HEADS):
            ref[hd] = fn(x[:, hd * V_DIM:(hd + 1) * V_DIM]).astype(BF16)

    head_major(jnp.dot(xb, w_ref[:, :d], preferred_element_type=F32), q_ref,
               lambda x: rope(x) * (LOG2_E * HEAD_DIM ** -0.5))
    head_major(jnp.dot(xb, w_ref[:, d:2 * d], preferred_element_type=F32), k_ref, rope)
    head_major(jnp.dot(xb, w_ref[:, 2 * d:], preferred_element_type=F32), v_ref, lambda x: x)


def _ffn(h, mods, norm_g, w_gu, w_down, layer, which, n_rows, row_of_tile, final_g=None, h_tail=None,
         attn=None, qkv=None, cast_next=()):
    d = h.shape[1]
    tm = ROW_TILE
    k0 = 6 * which
    final = final_g is not None
    n_first = None
    in_specs, args = [pl.BlockSpec((tm, d), lambda t: (t, 0))], [h]
    if h_tail is not None:
        n_first = h.shape[0] // tm
        in_specs = [pl.BlockSpec((tm, d), lambda t: (jnp.minimum(t, n_first - 1), 0)),
                    pl.BlockSpec((tm, d), lambda t: (jnp.maximum(t - n_first, 0), 0))]
        args = [h, h_tail]
    head_major = pl.BlockSpec((N_HEADS, tm, V_DIM), lambda t: (0, t, 0))
    if attn is not None:
        o_hm, w_out = attn
        in_specs += [head_major, _mod_spec(layer, 5, d, row_of_tile), _stacked(w_out, (), single_buffer=True)]
        args += [o_hm, mods, w_out]
    in_specs += [
        _mod_spec(layer, k0, d, row_of_tile),
        _mod_spec(layer, k0 + 1, d, row_of_tile),
        _mod_spec(layer, k0 + 2, d, row_of_tile),
        _stacked(norm_g, (layer, 2 * which)),
        _stacked(w_gu, (), single_buffer=True),
        _stacked(w_down, (), single_buffer=True),
    ]
    args += [mods, mods, mods, norm_g, w_gu, w_down]
    if final:
        in_specs.append(pl.BlockSpec((1, d), lambda t: (0, 0)))
        args.append(final_g.reshape(1, d))
    out_specs = [pl.BlockSpec((tm, d), lambda t: (t, 0))]
    out_shape = [jax.ShapeDtypeStruct((n_rows, d), F32)]
    if qkv is not None:
        w_in, cos_t, sin_t, rope_block = qkv
        in_specs += [_mod_spec(layer, 3, d, row_of_tile), _mod_spec(layer, 4, d, row_of_tile),
                     _stacked(norm_g, (layer, 1)), _stacked(w_in, (), single_buffer=True),
                     pl.BlockSpec((tm, V_DIM), lambda t: (rope_block(t), 0)),
                     pl.BlockSpec((tm, V_DIM), lambda t: (rope_block(t), 0))]
        args += [mods, mods, norm_g, w_in, cos_t, sin_t]
        out_specs += [head_major] * 3
        out_shape += [jax.ShapeDtypeStruct((N_HEADS, n_rows, V_DIM), BF16)] * 3
    n_steps = n_rows // tm
    for w32, idx in cast_next:
        rows, cols = w32.shape[len(idx):]
        n_blk = max(n for n in range(1, n_steps + 1)
                    if rows % n == 0 and (rows // n) % V7X_BF16_SUBLANES == 0)
        blk = rows // n_blk
        in_specs.append(pl.BlockSpec((None,) * len(idx) + (blk, cols),
                                     lambda t, idx=idx, n_blk=n_blk: idx + (jnp.minimum(t, n_blk - 1), 0)))
        args.append(w32)
        out_specs.append(pl.BlockSpec((blk, cols), lambda t, n_blk=n_blk: (jnp.minimum(t, n_blk - 1), 0)))
        out_shape.append(jax.ShapeDtypeStruct((rows, cols), BF16))
    return pl.pallas_call(
        functools.partial(_ffn_kernel, n_first=n_first, attn=attn is not None, final=final,
                          qkv=qkv is not None, n_cast=len(cast_next)),
        grid=(n_steps,),
        in_specs=in_specs,
        out_specs=out_specs,
        out_shape=out_shape,
        compiler_params=_params(1),
        name="ffn",
    )(*args)


def _attn_kernel(q_ref, ql_ref, qc_ref, kl_ref, kc_ref, vl_ref, vc_ref, lam_ref, sg_ref, o_ref, bounded_scr, *,
                 lam_init, ctx_tile):
    n_heads, tq, _ = q_ref.shape
    lv = lam_ref[...]
    lam = (jnp.exp(jnp.sum(lv[0:1] * lv[1:2], axis=-1, keepdims=True))
           - jnp.exp(jnp.sum(lv[2:3] * lv[3:4], axis=-1, keepdims=True)) + lam_init)
    lane = lax.broadcasted_iota(jnp.int32, (tq, V_DIM), 1)
    zero = jnp.zeros((tq, V_DIM), BF16)

    gs_row = lax.broadcasted_iota(jnp.int32, (V_DIM, V_DIM), 0)
    gs_col = lax.broadcasted_iota(jnp.int32, (V_DIM, V_DIM), 1)
    group_sum = jnp.where(gs_row // HEAD_DIM == gs_col, 1.0, 0.0).astype(BF16)

    def max_sq_norm(*xs):
        return functools.reduce(jnp.maximum, [
            jnp.max(jnp.dot(x * x, group_sum, preferred_element_type=F32), axis=0, keepdims=True) for x in xs])

    @pl.when(pl.program_id(2) == 0)
    def _():
        bound_sq = jnp.max(functools.reduce(jnp.maximum, [
            max_sq_norm(ql_ref[hd], qc_ref[hd]) * max_sq_norm(kl_ref[hd], kc_ref[hd]) for hd in range(n_heads)]))
        bounded_scr[0] = (bound_sq * NORM_MARGIN <= MAX_UNSHIFTED_SCORE ** 2).astype(jnp.int32)

    scores_bounded = bounded_scr[0] == 1
    nt = (((1,), (1,)), ((), ()))

    def attend(latent_keys, subtract_max):
        c = kc_ref.shape[1]

        def scores(hd):
            q = q_ref[hd]
            qs = jnp.concatenate([jnp.where(lane < HEAD_DIM, q, zero),
                                  jnp.where(lane >= HEAD_DIM, q, zero)], axis=0)
            s = lax.dot_general(qs, kc_ref[hd], nt, preferred_element_type=F32)
            if latent_keys:
                s = jnp.concatenate([s, lax.dot_general(qs, kl_ref[hd], nt, preferred_element_type=F32)],
                                    axis=-1)
            return s

        all_scores = [scores(hd) for hd in range(n_heads)]
        for hd in range(n_heads):
            s = all_scores[hd]
            if subtract_max:
                s = s - jnp.max(s, axis=-1, keepdims=True)
            e = jnp.exp2(s)
            inv = 1.0 / jnp.sum(e, axis=-1, keepdims=True)
            w = (e[:tq] - e[tq:] * (lam * inv[tq:] / inv[:tq])).astype(BF16)
            o = jnp.dot(w[:, :c], vc_ref[hd], preferred_element_type=F32)
            if latent_keys:
                o = o + jnp.dot(w[:, c:], vl_ref[hd], preferred_element_type=F32)
            o = _rms(o * inv[:tq], sg_ref[...]) * (1.0 - lam_init)
            o_ref[hd] = o.astype(BF16)

    def attend_guarded(latent_keys):
        pl.when(scores_bounded)(lambda: attend(latent_keys, subtract_max=False))
        pl.when(jnp.logical_not(scores_bounded))(lambda: attend(latent_keys, subtract_max=True))

    if ctx_tile:
        is_ctx = pl.program_id(2) == 0
        pl.when(is_ctx)(lambda: attend_guarded(False))
        pl.when(jnp.logical_not(is_ctx))(lambda: attend_guarded(True))
    else:
        attend_guarded(True)


def _attention(q, k, v, lam_vecs, subln_g, lam_init, j, dims, context_queries):
    b, s, c = dims
    tq = ATTN_TILE
    assert c == tq
    lat_per_b = s // tq
    ctx_blocks_from = b * s // c
    if context_queries:
        n_qt, out_rows = lat_per_b + 1, b * (s + c)
        q_block = lambda bb, jq: jnp.where(jq == 0, ctx_blocks_from + bb, bb * lat_per_b + jq - 1)
    else:
        n_qt, out_rows = lat_per_b, b * s
        q_block = lambda bb, jq: bb * lat_per_b + jq
    hb = HEADS_PER_STEP
    lat_spec = pl.BlockSpec((hb, s, V_DIM), lambda bb, hg, jq: (hg, bb, 0))
    ctx_spec = pl.BlockSpec((hb, c, V_DIM), lambda bb, hg, jq: (hg, ctx_blocks_from + bb, 0))
    tile_spec = pl.BlockSpec((hb, tq, V_DIM), lambda bb, hg, jq: (hg, q_block(bb, jq), 0))
    return pl.pallas_call(
        functools.partial(_attn_kernel, lam_init=lam_init, ctx_tile=context_queries),
        grid=(b, N_HEADS // hb, n_qt),
        in_specs=[tile_spec, lat_spec, ctx_spec, lat_spec, ctx_spec, lat_spec, ctx_spec,
                  _stacked(lam_vecs, (j,)), _stacked(subln_g, (j,))],
        out_specs=tile_spec,
        out_shape=jax.ShapeDtypeStruct((N_HEADS, out_rows, V_DIM), BF16),
        scratch_shapes=[pltpu.SMEM((1,), jnp.int32)],
        compiler_params=_params(3),
        name="attention",
    )(q, q, q, k, k, v, v, lam_vecs, subln_g)


def _sg_kernel(h_ref, sh_ref, sc_ref, gt_ref, g_ref, win_ref, lng_ref, lnb_ref, ws_ref, bs_ref, wout_ref,
               o_ref):
    tm = h_ref.shape[0]
    e = wout_ref.shape[0]
    gd = e // SG_GROUPS
    h = h_ref[...]
    xb = _modulated_bf16(h, g_ref[...], sh_ref[...], sc_ref[...])
    u = _gelu_tanh(jnp.dot(xb, win_ref[:, :e], preferred_element_type=F32))
    v = _gelu_tanh(jnp.dot(xb, win_ref[:, e:], preferred_element_type=F32))
    mu = jnp.mean(v, axis=-1, keepdims=True)
    vc = v - mu
    vn = vc * lax.rsqrt(jnp.mean(vc * vc, axis=-1, keepdims=True) + LN_EPS)
    vb = (vn * lng_ref[...] + lnb_ref[...]).astype(BF16)
    bias = bs_ref[...]
    rows = []
    for n in range(tm // CHUNK):
        r = slice(n * CHUNK, (n + 1) * CHUNK)
        cols = [jnp.dot(ws_ref[gi], vb[r, gi * gd:(gi + 1) * gd], preferred_element_type=F32)
                for gi in range(SG_GROUPS)]
        rows.append(jnp.concatenate(cols, axis=-1) + bias)
    mixed = jnp.concatenate(rows, axis=0)
    y = jnp.dot((u * mixed).astype(BF16), wout_ref[...], preferred_element_type=F32)
    o_ref[...] = h + gt_ref[...] * y


def _sg_mixer(h, mods, norm_g, w_in, ln_g, ln_b, w_s, bias_full, w_out, layer, j, n_rows, row_of_tile):
    d = h.shape[1]
    tm = ROW_TILE
    return pl.pallas_call(
        _sg_kernel,
        grid=(n_rows // tm,),
        in_specs=[
            pl.BlockSpec((tm, d), lambda t: (t, 0)),
            _mod_spec(layer, 3, d, row_of_tile),
            _mod_spec(layer, 4, d, row_of_tile),
            _mod_spec(layer, 5, d, row_of_tile),
            _stacked(norm_g, (layer, 1)),
            _stacked(w_in, (), single_buffer=True),
            _stacked(ln_g, (j,)),
            _stacked(ln_b, (j,)),
            _stacked(w_s, (j,)),
            _stacked(bias_full, (j,)),
            _stacked(w_out, (), single_buffer=True),
        ],
        out_specs=pl.BlockSpec((tm, d), lambda t: (t, 0)),
        out_shape=jax.ShapeDtypeStruct((n_rows, d), F32),
        compiler_params=_params(1),
        name="sg_mixer",
    )(h, mods, mods, mods, norm_g, w_in, ln_g, ln_b, w_s, bias_full, w_out)


def _rope_tables(s, tm):
    rows_n = s // GRID_W
    row = jnp.repeat(jnp.arange(rows_n), GRID_W)
    col = jnp.tile(jnp.arange(GRID_W), rows_n)
    n_freq = HEAD_DIM // 4
    inv = ROPE_THETA ** (-jnp.arange(n_freq, dtype=F32) / n_freq)
    pos = jnp.stack([row, col], axis=-1).astype(F32)
    ang = pos[:, :, None] * inv
    cos, sin = jnp.cos(ang), jnp.sin(ang)
    cos_l = jnp.broadcast_to(cos[:, None, :, None, :], (s, 2, 2, 2, n_freq)).reshape(s, V_DIM)
    sign = jnp.array([-1.0, 1.0], F32)[None, None, None, :, None]
    sin_l = jnp.broadcast_to(sin[:, None, :, None, :] * sign, (s, 2, 2, 2, n_freq)).reshape(s, V_DIM)
    cos_t = jnp.concatenate([cos_l, jnp.ones((tm, V_DIM), F32)], axis=0)
    sin_t = jnp.concatenate([sin_l, jnp.zeros((tm, V_DIM), F32)], axis=0)
    return cos_t, sin_t


def kernel(x, c, ctx, c_ctx, w_mod, b_mod, norm_g, w_ffn_gu, w_ffn_down, da_w_in, da_w_out, da_lambda,
           da_subln_g, sg_w_in, sg_ln_g, sg_ln_b, sg_w_s, sg_b_s, sg_w_out, final_g):
    b, s, d = x.shape
    cl = ctx.shape[1]
    depth = w_mod.shape[0]
    e = sg_w_out.shape[1]
    assert cl == ATTN_TILE and s % ROW_TILE == 0 and s % GRID_W == 0 and b < MOD_ROWS
    assert (b * cl) % ROW_TILE == 0 and ROW_TILE % CHUNK == 0 and cl % CHUNK == 0
    dims = (b, s, cl)
    n_lat, n_all = b * s, b * (s + cl)
    tiles_per_b = s // ROW_TILE

    def row_of_tile(t):
        return jnp.minimum(t // tiles_per_b, b)

    def rope_block(t):
        return jnp.where(t < n_lat // ROW_TILE, t % tiles_per_b, tiles_per_b)

    c_all = jnp.zeros((MOD_ROWS, d), F32).at[:b].set(c).at[b].set(c_ctx)
    mods = _mod_table(c_all, w_mod, b_mod).reshape(depth, MOD_ROWS, N_MOD, 1, d)
    cos_t, sin_t = _rope_tables(s, ROW_TILE)

    norm_g4 = norm_g.reshape(depth, 3, 1, d)
    ffn_w = [w_ffn_gu[0, 0].astype(BF16), w_ffn_down[0, 0].astype(BF16)]

    def mixer_weights(i):
        w_in, w_out = (da_w_in, da_w_out) if i % N_MIXERS == 0 else (sg_w_in, sg_w_out)
        return (w_in, (i // N_MIXERS,)), (w_out, (i // N_MIXERS,))

    mix_w = [w[idx].astype(BF16) for w, idx in mixer_weights(0)]
    g_ws = sg_w_s.astype(BF16)
    subln_g = da_subln_g.reshape(-1, 1, V_DIM)
    ln_g, ln_b = sg_ln_g.reshape(-1, 1, e), sg_ln_b.reshape(-1, 1, e)
    bias_full = jnp.repeat(jnp.swapaxes(sg_b_s, 1, 2), e // SG_GROUPS, axis=2)

    last_ctx_layer = max(i for i in range(depth) if i % N_MIXERS == 0)
    h = x.reshape(n_lat, d)
    for i in range(depth):
        j = i // N_MIXERS
        attention_layer = i % N_MIXERS == 0
        rows_in = n_all if i <= last_ctx_layer else n_lat
        rows_out = n_all if i < last_ctx_layer else n_lat
        h_tail = ctx.reshape(b * cl, d) if i == 0 and rows_in == n_all else None
        assert not attention_layer or rows_in == n_all
        h, *rest = _ffn(h, mods, norm_g4, *ffn_w, i, 0, rows_in, row_of_tile, h_tail=h_tail,
                        qkv=(mix_w[0], cos_t, sin_t, rope_block) if attention_layer else None,
                        cast_next=((w_ffn_gu, (i, 1)), (w_ffn_down, (i, 1))))
        ffn_w = rest[-2:]
        attn = None
        if attention_layer:
            q, k, v = rest[:3]
            lam_init = 0.8 - 0.6 * math.exp(-0.3 * i)
            o = _attention(q, k, v, da_lambda, subln_g, lam_init, j, dims,
                           context_queries=i < last_ctx_layer)
            attn = (o, mix_w[1])
        else:
            h = _sg_mixer(h, mods, norm_g4, mix_w[0], ln_g, ln_b, g_ws, bias_full, mix_w[1], i, j, rows_out,
                          row_of_tile)
        last = i == depth - 1
        cast_next = () if last else (((w_ffn_gu, (i + 1, 0)), (w_ffn_down, (i + 1, 0)))
                                     + mixer_weights(i + 1))
        h, *cast = _ffn(h, mods, norm_g4, *ffn_w, i, 1, rows_out, row_of_tile,
                        final_g=final_g if last else None, attn=attn, cast_next=cast_next)
        ffn_w, mix_w = cast[:2], cast[2:]
    return h.reshape(b, s, d)
```

```python
import functools
import math

import jax
import jax.numpy as jnp
from jax import lax
from jax.experimental import pallas as pl
from jax.experimental.pallas import tpu as pltpu

F32 = jnp.float32
BF16 = jnp.bfloat16

GRID_W = 64
N_MIXERS = 2
N_HEADS = 8
HEAD_DIM = 64
V_DIM = 2 * HEAD_DIM
ROPE_THETA = 10000.0
CHUNK = 128
SG_GROUPS = 8
N_MOD = 9
RMS_EPS = 1e-6
LN_EPS = 1e-5

V7X_LANES = 128
V7X_BF16_SUBLANES = 16
V7X_VMEM_BYTES = 64 * 1024 * 1024
VMEM_LIMIT_BYTES = V7X_VMEM_BYTES - 8 * 1024 * 1024

MOD_ROWS = 16
MOD_COL_BLOCK = 3072
ATTN_TILE = 256
ROW_TILE = 512
FFN_ROW_CHAINS = 2
HEADS_PER_STEP = 4
LOG2_E = math.log2(math.e)
MAX_UNSHIFTED_SCORE = 40.0
NORM_MARGIN = 1.05


def _params(n_axes):
    return pltpu.CompilerParams(dimension_semantics=("arbitrary",) * n_axes,
                                vmem_limit_bytes=VMEM_LIMIT_BYTES)


def _rms(x, g):
    y = x * lax.rsqrt(jnp.mean(x * x, axis=-1, keepdims=True) + RMS_EPS)
    return y * g


def _modulated_bf16(h, g, shift, scale):
    r = lax.rsqrt(jnp.mean(h * h, axis=-1, keepdims=True) + RMS_EPS)
    return ((h * r) * (g * (1.0 + scale)) + shift).astype(BF16)


def _gelu_tanh(x):
    c1 = -2.0 * LOG2_E * math.sqrt(2.0 / math.pi)
    return x / (1.0 + jnp.exp2(x * (c1 + (0.044715 * c1) * (x * x))))


def _stacked(arr, idx, single_buffer=False):
    tail = arr.shape[len(idx):]
    index = tuple(idx) + (0,) * len(tail)
    kwargs = dict(pipeline_mode=pl.Buffered(1)) if single_buffer else {}
    return pl.BlockSpec((None,) * len(idx) + tail, lambda *_: index, **kwargs)


def _mod_spec(layer, k, d, row_of_tile):
    return pl.BlockSpec((None, None, None, 1, d), lambda t: (layer, row_of_tile(t), k, 0, 0))


def _mod_kernel(c_ref, w_ref, b_ref, o_ref):
    sc = jax.nn.silu(c_ref[...]).astype(BF16)
    o_ref[...] = jnp.dot(sc, w_ref[...].astype(BF16), preferred_element_type=F32) + b_ref[...]


def _mod_table(c_all, w_mod, b_mod):
    depth, d, nd = w_mod.shape
    tn = MOD_COL_BLOCK
    assert nd % tn == 0
    return pl.pallas_call(
        _mod_kernel,
        grid=(depth, nd // tn),
        in_specs=[
            pl.BlockSpec((MOD_ROWS, d), lambda i, n: (0, 0)),
            pl.BlockSpec((None, d, tn), lambda i, n: (i, 0, n)),
            pl.BlockSpec((None, 1, tn), lambda i, n: (i, 0, n)),
        ],
        out_specs=pl.BlockSpec((None, MOD_ROWS, tn), lambda i, n: (i, 0, n)),
        out_shape=jax.ShapeDtypeStruct((depth, MOD_ROWS, nd), F32),
        compiler_params=_params(2),
        name="mod_table",
    )(c_all, w_mod, b_mod.reshape(depth, 1, nd))


def _ffn_kernel(*refs, n_first, attn, final, qkv, n_cast):
    refs = list(refs)
    cast_out = [refs.pop() for _ in range(n_cast)][::-1]
    qkv_out = [refs.pop() for _ in range(3 if qkv else 0)][::-1]
    o_ref = refs.pop()
    cast_in = [refs.pop() for _ in range(n_cast)][::-1]
    for src_ref, dst_ref in zip(cast_in, cast_out):
        dst_ref[...] = src_ref[...].astype(BF16)
    qkv_in = [refs.pop() for _ in range(6 if qkv else 0)][::-1]
    fg_ref = refs.pop() if final else None
    h_ref = refs.pop(0)
    h = h_ref[...]
    if n_first is not None:
        h = jnp.where(pl.program_id(0) < n_first, h, refs.pop(0)[...])
    if attn:
        attn_ref, mixer_gate_ref, wout_ref = refs.pop(0), refs.pop(0), refs.pop(0)
        heads = jnp.concatenate([attn_ref[hd] for hd in range(N_HEADS)], axis=-1)
        h = h + mixer_gate_ref[...] * jnp.dot(heads, wout_ref[...], preferred_element_type=F32)
    sh_ref, sc_ref, gt_ref, g_ref, wgu_ref, wd_ref = refs
    d_ff = wd_ref.shape[0]
    xb = _modulated_bf16(h, g_ref[...], sh_ref[...], sc_ref[...])
    rows = h.shape[0] // FFN_ROW_CHAINS
    ys = []
    for part in range(FFN_ROW_CHAINS):
        xp = xb[part * rows:(part + 1) * rows]
        a = jnp.dot(xp, wgu_ref[:, :d_ff], preferred_element_type=F32)
        u = jnp.dot(xp, wgu_ref[:, d_ff:], preferred_element_type=F32)
        act = (jax.nn.silu(a) * u).astype(BF16)
        ys.append(jnp.dot(act, wd_ref[...], preferred_element_type=F32))
    y = jnp.concatenate(ys, axis=0)
    out = h + (0.5 * gt_ref[...]) * y
    if final:
        out = _rms(out, fg_ref[...])
    o_ref[...] = out
    if qkv:
        msh_ref, msc_ref, mg_ref, win_ref, cos_ref, sin_ref = qkv_in
        xq = _modulated_bf16(out, mg_ref[...], msh_ref[...], msc_ref[...])
        _write_qkv(xq, win_ref, cos_ref[...], sin_ref[...], *qkv_out)


def _write_qkv(xb, w_ref, cos, sin, q_ref, k_ref, v_ref):
    d = xb.shape[1]
    lane = lax.broadcasted_iota(jnp.int32, cos.shape, 1)
    first_half = (lane % (HEAD_DIM // 2)) < (HEAD_DIM // 4)

    def rope(x):
        partner = jnp.where(first_half,
                            pltpu.roll(x, V7X_LANES - HEAD_DIM // 4, axis=1),
                            pltpu.roll(x, HEAD_DIM // 4, axis=1))
        return x * cos + partner * sin

    def head_major(x, ref, fn):
        for hd in range(N_HEADS):
            ref[hd] = fn(x[:, hd * V_DIM:(hd + 1) * V_DIM]).astype(BF16)

    head_major(jnp.dot(xb, w_ref[:, :d], preferred_element_type=F32), q_ref,
               lambda x: rope(x) * (LOG2_E * HEAD_DIM ** -0.5))
    head_major(jnp.dot(xb, w_ref[:, d:2 * d], preferred_element_type=F32), k_ref, rope)
    head_major(jnp.dot(xb, w_ref[:, 2 * d:], preferred_element_type=F32), v_ref, lambda x: x)


def _ffn(h, mods, norm_g, w_gu, w_down, layer, which, n_rows, row_of_tile, final_g=None, h_tail=None,
         attn=None, qkv=None, cast_next=()):
    d = h.shape[1]
    tm = ROW_TILE
    k0 = 6 * which
    final = final_g is not None
    n_first = None
    in_specs, args = [pl.BlockSpec((tm, d), lambda t: (t, 0))], [h]
    if h_tail is not None:
        n_first = h.shape[0] // tm
        in_specs = [pl.BlockSpec((tm, d), lambda t: (jnp.minimum(t, n_first - 1), 0)),
                    pl.BlockSpec((tm, d), lambda t: (jnp.maximum(t - n_first, 0), 0))]
        args = [h, h_tail]
    head_major = pl.BlockSpec((N_HEADS, tm, V_DIM), lambda t: (0, t, 0))
    if attn is not None:
        o_hm, w_out = attn
        in_specs += [head_major, _mod_spec(layer, 5, d, row_of_tile), _stacked(w_out, (), single_buffer=True)]
        args += [o_hm, mods, w_out]
    in_specs += [
        _mod_spec(layer, k0, d, row_of_tile),
        _mod_spec(layer, k0 + 1, d, row_of_tile),
        _mod_spec(layer, k0 + 2, d, row_of_tile),
        _stacked(norm_g, (layer, 2 * which)),
        _stacked(w_gu, (), single_buffer=True),
        _stacked(w_down, (), single_buffer=True),
    ]
    args += [mods, mods, mods, norm_g, w_gu, w_down]
    if final:
        in_specs.append(pl.BlockSpec((1, d), lambda t: (0, 0)))
        args.append(final_g.reshape(1, d))
    out_specs = [pl.BlockSpec((tm, d), lambda t: (t, 0))]
    out_shape = [jax.ShapeDtypeStruct((n_rows, d), F32)]
    if qkv is not None:
        w_in, cos_t, sin_t, rope_block = qkv
        in_specs += [_mod_spec(layer, 3, d, row_of_tile), _mod_spec(layer, 4, d, row_of_tile),
                     _stacked(norm_g, (layer, 1)), _stacked(w_in, (), single_buffer=True),
                     pl.BlockSpec((tm, V_DIM), lambda t: (rope_block(t), 0)),
                     pl.BlockSpec((tm, V_DIM), lambda t: (rope_block(t), 0))]
        args += [mods, mods, norm_g, w_in, cos_t, sin_t]
        out_specs += [head_major] * 3
        out_shape += [jax.ShapeDtypeStruct((N_HEADS, n_rows, V_DIM), BF16)] * 3
    n_steps = n_rows // tm
    for w32, idx in cast_next:
        rows, cols = w32.shape[len(idx):]
        n_blk = max(n for n in range(1, n_steps + 1)
                    if rows % n == 0 and (rows // n) % V7X_BF16_SUBLANES == 0)
        blk = rows // n_blk
        in_specs.append(pl.BlockSpec((None,) * len(idx) + (blk, cols),
                                     lambda t, idx=idx, n_blk=n_blk: idx + (jnp.minimum(t, n_blk - 1), 0)))
        args.append(w32)
        out_specs.append(pl.BlockSpec((blk, cols), lambda t, n_blk=n_blk: (jnp.minimum(t, n_blk - 1), 0)))
        out_shape.append(jax.ShapeDtypeStruct((rows, cols), BF16))
    return pl.pallas_call(
        functools.partial(_ffn_kernel, n_first=n_first, attn=attn is not None, final=final,
                          qkv=qkv is not None, n_cast=len(cast_next)),
        grid=(n_steps,),
        in_specs=in_specs,
        out_specs=out_specs,
        out_shape=out_shape,
        compiler_params=_params(1),
        name="ffn",
    )(*args)


def _attn_kernel(q_ref, ql_ref, qc_ref, kl_ref, kc_ref, vl_ref, vc_ref, lam_ref, sg_ref, o_ref, bounded_scr, *,
                 lam_init, ctx_tile):
    n_heads, tq, _ = q_ref.shape
    lv = lam_ref[...]
    lam = (jnp.exp(jnp.sum(lv[0:1] * lv[1:2], axis=-1, keepdims=True))
           - jnp.exp(jnp.sum(lv[2:3] * lv[3:4], axis=-1, keepdims=True)) + lam_init)
    lane = lax.broadcasted_iota(jnp.int32, (tq, V_DIM), 1)
    zero = jnp.zeros((tq, V_DIM), BF16)

    gs_row = lax.broadcasted_iota(jnp.int32, (V_DIM, V_DIM), 0)
    gs_col = lax.broadcasted_iota(jnp.int32, (V_DIM, V_DIM), 1)
    group_sum = jnp.where(gs_row // HEAD_DIM == gs_col, 1.0, 0.0).astype(BF16)

    def max_sq_norm(*xs):
        return functools.reduce(jnp.maximum, [
            jnp.max(jnp.dot(x * x, group_sum, preferred_element_type=F32), axis=0, keepdims=True) for x in xs])

    @pl.when(pl.program_id(2) == 0)
    def _():
        bound_sq = jnp.max(functools.reduce(jnp.maximum, [
            max_sq_norm(ql_ref[hd], qc_ref[hd]) * max_sq_norm(kl_ref[hd], kc_ref[hd]) for hd in range(n_heads)]))
        bounded_scr[0] = (bound_sq * NORM_MARGIN <= MAX_UNSHIFTED_SCORE ** 2).astype(jnp.int32)

    scores_bounded = bounded_scr[0] == 1
    nt = (((1,), (1,)), ((), ()))

    def attend(latent_keys, subtract_max):
        c = kc_ref.shape[1]

        def scores(hd):
            q = q_ref[hd]
            qs = jnp.concatenate([jnp.where(lane < HEAD_DIM, q, zero),
                                  jnp.where(lane >= HEAD_DIM, q, zero)], axis=0)
            s = lax.dot_general(qs, kc_ref[hd], nt, preferred_element_type=F32)
            if latent_keys:
                s = jnp.concatenate([s, lax.dot_general(qs, kl_ref[hd], nt, preferred_element_type=F32)],
                                    axis=-1)
            return s

        all_scores = [scores(hd) for hd in range(n_heads)]
        for hd in range(n_heads):
            s = all_scores[hd]
            if subtract_max:
                s = s - jnp.max(s, axis=-1, keepdims=True)
            e = jnp.exp2(s)
            inv = 1.0 / jnp.sum(e, axis=-1, keepdims=True)
            w = (e[:tq] - e[tq:] * (lam * inv[tq:] / inv[:tq])).astype(BF16)
            o = jnp.dot(w[:, :c], vc_ref[hd], preferred_element_type=F32)
            if latent_keys:
                o = o + jnp.dot(w[:, c:], vl_ref[hd], preferred_element_type=F32)
            o = _rms(o * inv[:tq], sg_ref[...]) * (1.0 - lam_init)
            o_ref[hd] = o.astype(BF16)

    def attend_guarded(latent_keys):
        pl.when(scores_bounded)(lambda: attend(latent_keys, subtract_max=False))
        pl.when(jnp.logical_not(scores_bounded))(lambda: attend(latent_keys, subtract_max=True))

    if ctx_tile:
        is_ctx = pl.program_id(2) == 0
        pl.when(is_ctx)(lambda: attend_guarded(False))
        pl.when(jnp.logical_not(is_ctx))(lambda: attend_guarded(True))
    else:
        attend_guarded(True)


def _attention(q, k, v, lam_vecs, subln_g, lam_init, j, dims, context_queries):
    b, s, c = dims
    tq = ATTN_TILE
    assert c == tq
    lat_per_b = s // tq
    ctx_blocks_from = b * s // c
    if context_queries:
        n_qt, out_rows = lat_per_b + 1, b * (s + c)
        q_block = lambda bb, jq: jnp.where(jq == 0, ctx_blocks_from + bb, bb * lat_per_b + jq - 1)
    else:
        n_qt, out_rows = lat_per_b, b * s
        q_block = lambda bb, jq: bb * lat_per_b + jq
    hb = HEADS_PER_STEP
    lat_spec = pl.BlockSpec((hb, s, V_DIM), lambda bb, hg, jq: (hg, bb, 0))
    ctx_spec = pl.BlockSpec((hb, c, V_DIM), lambda bb, hg, jq: (hg, ctx_blocks_from + bb, 0))
    tile_spec = pl.BlockSpec((hb, tq, V_DIM), lambda bb, hg, jq: (hg, q_block(bb, jq), 0))
    return pl.pallas_call(
        functools.partial(_attn_kernel, lam_init=lam_init, ctx_tile=context_queries),
        grid=(b, N_HEADS // hb, n_qt),
        in_specs=[tile_spec, lat_spec, ctx_spec, lat_spec, ctx_spec, lat_spec, ctx_spec,
                  _stacked(lam_vecs, (j,)), _stacked(subln_g, (j,))],
        out_specs=tile_spec,
        out_shape=jax.ShapeDtypeStruct((N_HEADS, out_rows, V_DIM), BF16),
        scratch_shapes=[pltpu.SMEM((1,), jnp.int32)],
        compiler_params=_params(3),
        name="attention",
    )(q, q, q, k, k, v, v, lam_vecs, subln_g)


def _sg_kernel(h_ref, sh_ref, sc_ref, gt_ref, g_ref, win_ref, lng_ref, lnb_ref, ws_ref, bs_ref, wout_ref,
               o_ref):
    tm = h_ref.shape[0]
    e = wout_ref.shape[0]
    gd = e // SG_GROUPS
    h = h_ref[...]
    xb = _modulated_bf16(h, g_ref[...], sh_ref[...], sc_ref[...])
    u = _gelu_tanh(jnp.dot(xb, win_ref[:, :e], preferred_element_type=F32))
    v = _gelu_tanh(jnp.dot(xb, win_ref[:, e:], preferred_element_type=F32))
    mu = jnp.mean(v, axis=-1, keepdims=True)
    vc = v - mu
    vn = vc * lax.rsqrt(jnp.mean(vc * vc, axis=-1, keepdims=True) + LN_EPS)
    vb = (vn * lng_ref[...] + lnb_ref[...]).astype(BF16)
    bias = bs_ref[...]
    rows = []
    for n in range(tm // CHUNK):
        r = slice(n * CHUNK, (n + 1) * CHUNK)
        cols = [jnp.dot(ws_ref[gi], vb[r, gi * gd:(gi + 1) * gd], preferred_element_type=F32)
                for gi in range(SG_GROUPS)]
        rows.append(jnp.concatenate(cols, axis=-1) + bias)
    mixed = jnp.concatenate(rows, axis=0)
    y = jnp.dot((u * mixed).astype(BF16), wout_ref[...], preferred_element_type=F32)
    o_ref[...] = h + gt_ref[...] * y


def _sg_mixer(h, mods, norm_g, w_in, ln_g, ln_b, w_s, bias_full, w_out, layer, j, n_rows, row_of_tile):
    d = h.shape[1]
    tm = ROW_TILE
    return pl.pallas_call(
        _sg_kernel,
        grid=(n_rows // tm,),
        in_specs=[
            pl.BlockSpec((tm, d), lambda t: (t, 0)),
            _mod_spec(layer, 3, d, row_of_tile),
            _mod_spec(layer, 4, d, row_of_tile),
            _mod_spec(layer, 5, d, row_of_tile),
            _stacked(norm_g, (layer, 1)),
            _stacked(w_in, (), single_buffer=True),
            _stacked(ln_g, (j,)),
            _stacked(ln_b, (j,)),
            _stacked(w_s, (j,)),
            _stacked(bias_full, (j,)),
            _stacked(w_out, (), single_buffer=True),
        ],
        out_specs=pl.BlockSpec((tm, d), lambda t: (t, 0)),
        out_shape=jax.ShapeDtypeStruct((n_rows, d), F32),
        compiler_params=_params(1),
        name="sg_mixer",
    )(h, mods, mods, mods, norm_g, w_in, ln_g, ln_b, w_s, bias_full, w_out)


def _rope_tables(s, tm):
    rows_n = s // GRID_W
    row = jnp.repeat(jnp.arange(rows_n), GRID_W)
    col = jnp.tile(jnp.arange(GRID_W), rows_n)
    n_freq = HEAD_DIM // 4
    inv = ROPE_THETA ** (-jnp.arange(n_freq, dtype=F32) / n_freq)
    pos = jnp.stack([row, col], axis=-1).astype(F32)
    ang = pos[:, :, None] * inv
    cos, sin = jnp.cos(ang), jnp.sin(ang)
    cos_l = jnp.broadcast_to(cos[:, None, :, None, :], (s, 2, 2, 2, n_freq)).reshape(s, V_DIM)
    sign = jnp.array([-1.0, 1.0], F32)[None, None, None, :, None]
    sin_l = jnp.broadcast_to(sin[:, None, :, None, :] * sign, (s, 2, 2, 2, n_freq)).reshape(s, V_DIM)
    cos_t = jnp.concatenate([cos_l, jnp.ones((tm, V_DIM), F32)], axis=0)
    sin_t = jnp.concatenate([sin_l, jnp.zeros((tm, V_DIM), F32)], axis=0)
    return cos_t, sin_t


def kernel(x, c, ctx, c_ctx, w_mod, b_mod, norm_g, w_ffn_gu, w_ffn_down, da_w_in, da_w_out, da_lambda,
           da_subln_g, sg_w_in, sg_ln_g, sg_ln_b, sg_w_s, sg_b_s, sg_w_out, final_g):
    b, s, d = x.shape
    cl = ctx.shape[1]
    depth = w_mod.shape[0]
    e = sg_w_out.shape[1]
    assert cl == ATTN_TILE and s % ROW_TILE == 0 and s % GRID_W == 0 and b < MOD_ROWS
    assert (b * cl) % ROW_TILE == 0 and ROW_TILE % CHUNK == 0 and cl % CHUNK == 0
    dims = (b, s, cl)
    n_lat, n_all = b * s, b * (s + cl)
    tiles_per_b = s // ROW_TILE

    def row_of_tile(t):
        return jnp.minimum(t // tiles_per_b, b)

    def rope_block(t):
        return jnp.where(t < n_lat // ROW_TILE, t % tiles_per_b, tiles_per_b)

    c_all = jnp.zeros((MOD_ROWS, d), F32).at[:b].set(c).at[b].set(c_ctx)
    mods = _mod_table(c_all, w_mod, b_mod).reshape(depth, MOD_ROWS, N_MOD, 1, d)
    cos_t, sin_t = _rope_tables(s, ROW_TILE)

    norm_g4 = norm_g.reshape(depth, 3, 1, d)
    ffn_w = [w_ffn_gu[0, 0].astype(BF16), w_ffn_down[0, 0].astype(BF16)]

    def mixer_weights(i):
        w_in, w_out = (da_w_in, da_w_out) if i % N_MIXERS == 0 else (sg_w_in, sg_w_out)
        return (w_in, (i // N_MIXERS,)), (w_out, (i // N_MIXERS,))

    mix_w = [w[idx].astype(BF16) for w, idx in mixer_weights(0)]
    g_ws = sg_w_s.astype(BF16)
    subln_g = da_subln_g.reshape(-1, 1, V_DIM)
    ln_g, ln_b = sg_ln_g.reshape(-1, 1, e), sg_ln_b.reshape(-1, 1, e)
    bias_full = jnp.repeat(jnp.swapaxes(sg_b_s, 1, 2), e // SG_GROUPS, axis=2)

    last_ctx_layer = max(i for i in range(depth) if i % N_MIXERS == 0)
    h = x.reshape(n_lat, d)
    for i in range(depth):
        j = i // N_MIXERS
        attention_layer = i % N_MIXERS == 0
        rows_in = n_all if i <= last_ctx_layer else n_lat
        rows_out = n_all if i < last_ctx_layer else n_lat
        h_tail = ctx.reshape(b * cl, d) if i == 0 and rows_in == n_all else None
        assert not attention_layer or rows_in == n_all
        h, *rest = _ffn(h, mods, norm_g4, *ffn_w, i, 0, rows_in, row_of_tile, h_tail=h_tail,
                        qkv=(mix_w[0], cos_t, sin_t, rope_block) if attention_layer else None,
                        cast_next=((w_ffn_gu, (i, 1)), (w_ffn_down, (i, 1))))
        ffn_w = rest[-2:]
        attn = None
        if attention_layer:
            q, k, v = rest[:3]
            lam_init = 0.8 - 0.6 * math.exp(-0.3 * i)
            o = _attention(q, k, v, da_lambda, subln_g, lam_init, j, dims,
                           context_queries=i < last_ctx_layer)
            attn = (o, mix_w[1])
        else:
            h = _sg_mixer(h, mods, norm_g4, mix_w[0], ln_g, ln_b, g_ws, bias_full, mix_w[1], i, j, rows_out,
                          row_of_tile)
        last = i == depth - 1
        cast_next = () if last else (((w_ffn_gu, (i + 1, 0)), (w_ffn_down, (i + 1, 0)))
                                     + mixer_weights(i + 1))
        h, *cast = _ffn(h, mods, norm_g4, *ffn_w, i, 1, rows_out, row_of_tile,
                        final_g=final_g if last else None, attn=attn, cast_next=cast_next)
        ffn_w, mix_w = cast[:2], cast[2:]
    return h.reshape(b, s, d)
```

```python
import functools
import math

import jax
import jax.numpy as jnp
from jax import lax
from jax.experimental import pallas as pl
from jax.experimental.pallas import tpu as pltpu

F32 = jnp.float32
BF16 = jnp.bfloat16

GRID_W = 64
N_MIXERS = 2
N_HEADS = 8
HEAD_DIM = 64
V_DIM = 2 * HEAD_DIM
ROPE_THETA = 10000.0
CHUNK = 128
SG_GROUPS = 8
N_MOD = 9
RMS_EPS = 1e-6
LN_EPS = 1e-5

V7X_LANES = 128
V7X_BF16_SUBLANES = 16
V7X_VMEM_BYTES = 64 * 1024 * 1024
VMEM_LIMIT_BYTES = V7X_VMEM_BYTES - 8 * 1024 * 1024

MOD_ROWS = 16
MOD_COL_BLOCK = 3072
ATTN_TILE = 256
ROW_TILE = 512
SG_ROW_TILE = 1024
FFN_ROW_CHAINS = 2
HEADS_PER_STEP = 4
LOG2_E = math.log2(math.e)
MAX_UNSHIFTED_SCORE = 40.0
NORM_MARGIN = 1.05


def _params(n_axes):
    return pltpu.CompilerParams(dimension_semantics=("arbitrary",) * n_axes,
                                vmem_limit_bytes=VMEM_LIMIT_BYTES)


def _rms(x, g):
    y = x * lax.rsqrt(jnp.mean(x * x, axis=-1, keepdims=True) + RMS_EPS)
    return y * g


def _modulated_bf16(h, g, shift, scale):
    r = lax.rsqrt(jnp.mean(h * h, axis=-1, keepdims=True) + RMS_EPS)
    return ((h * r) * (g * (1.0 + scale)) + shift).astype(BF16)


def _gelu_tanh(x):
    c1 = -2.0 * LOG2_E * math.sqrt(2.0 / math.pi)
    return x / (1.0 + jnp.exp2(x * (c1 + (0.044715 * c1) * (x * x))))


def _stacked(arr, idx, single_buffer=False):
    tail = arr.shape[len(idx):]
    index = tuple(idx) + (0,) * len(tail)
    kwargs = dict(pipeline_mode=pl.Buffered(1)) if single_buffer else {}
    return pl.BlockSpec((None,) * len(idx) + tail, lambda *_: index, **kwargs)


def _mod_spec(layer, k, d, row_of_tile):
    return pl.BlockSpec((None, None, None, 1, d), lambda t: (layer, row_of_tile(t), k, 0, 0))


def _mod_kernel(c_ref, w_ref, b_ref, o_ref):
    sc = jax.nn.silu(c_ref[...]).astype(BF16)
    o_ref[...] = jnp.dot(sc, w_ref[...].astype(BF16), preferred_element_type=F32) + b_ref[...]


def _mod_table(c_all, w_mod, b_mod):
    depth, d, nd = w_mod.shape
    tn = MOD_COL_BLOCK
    assert nd % tn == 0
    return pl.pallas_call(
        _mod_kernel,
        grid=(depth, nd // tn),
        in_specs=[
            pl.BlockSpec((MOD_ROWS, d), lambda i, n: (0, 0)),
            pl.BlockSpec((None, d, tn), lambda i, n: (i, 0, n)),
            pl.BlockSpec((None, 1, tn), lambda i, n: (i, 0, n)),
        ],
        out_specs=pl.BlockSpec((None, MOD_ROWS, tn), lambda i, n: (i, 0, n)),
        out_shape=jax.ShapeDtypeStruct((depth, MOD_ROWS, nd), F32),
        compiler_params=_params(2),
        name="mod_table",
    )(c_all, w_mod, b_mod.reshape(depth, 1, nd))


def _ffn_kernel(*refs, n_first, attn, final, qkv, n_cast):
    refs = list(refs)
    cast_out = [refs.pop() for _ in range(n_cast)][::-1]
    qkv_out = [refs.pop() for _ in range(3 if qkv else 0)][::-1]
    o_ref = refs.pop()
    cast_in = [refs.pop() for _ in range(n_cast)][::-1]
    for src_ref, dst_ref in zip(cast_in, cast_out):
        dst_ref[...] = src_ref[...].astype(BF16)
    qkv_in = [refs.pop() for _ in range(6 if qkv else 0)][::-1]
    fg_ref = refs.pop() if final else None
    h_ref = refs.pop(0)
    h = h_ref[...]
    if n_first is not None:
        h = jnp.where(pl.program_id(0) < n_first, h, refs.pop(0)[...])
    if attn:
        attn_ref, mixer_gate_ref, wout_ref = refs.pop(0), refs.pop(0), refs.pop(0)
        heads = jnp.concatenate([attn_ref[hd] for hd in range(N_HEADS)], axis=-1)
        h = h + mixer_gate_ref[...] * jnp.dot(heads, wout_ref[...], preferred_element_type=F32)
    sh_ref, sc_ref, gt_ref, g_ref, wgu_ref, wd_ref = refs
    d_ff = wd_ref.shape[0]
    xb = _modulated_bf16(h, g_ref[...], sh_ref[...], sc_ref[...])
    rows = h.shape[0] // FFN_ROW_CHAINS
    ys = []
    for part in range(FFN_ROW_CHAINS):
        xp = xb[part * rows:(part + 1) * rows]
        a = jnp.dot(xp, wgu_ref[:, :d_ff], preferred_element_type=F32)
        u = jnp.dot(xp, wgu_ref[:, d_ff:], preferred_element_type=F32)
        act = (jax.nn.silu(a) * u).astype(BF16)
        ys.append(jnp.dot(act, wd_ref[...], preferred_element_type=F32))
    y = jnp.concatenate(ys, axis=0)
    out = h + (0.5 * gt_ref[...]) * y
    if final:
        out = _rms(out, fg_ref[...])
    o_ref[...] = out
    if qkv:
        msh_ref, msc_ref, mg_ref, win_ref, cos_ref, sin_ref = qkv_in
        xq = _modulated_bf16(out, mg_ref[...], msh_ref[...], msc_ref[...])
        _write_qkv(xq, win_ref, cos_ref[...], sin_ref[...], *qkv_out)


def _write_qkv(xb, w_ref, cos, sin, q_ref, k_ref, v_ref):
    d = xb.shape[1]
    lane = lax.broadcasted_iota(jnp.int32, cos.shape, 1)
    first_half = (lane % (HEAD_DIM // 2)) < (HEAD_DIM // 4)

    def rope(x):
        partner = jnp.where(first_half,
                            pltpu.roll(x, V7X_LANES - HEAD_DIM // 4, axis=1),
                            pltpu.roll(x, HEAD_DIM // 4, axis=1))
        return x * cos + partner * sin

    def head_major(x, ref, fn):
        for hd in range(N_HEADS):
            ref[hd] = fn(x[:, hd * V_DIM:(hd + 1) * V_DIM]).astype(BF16)

    head_major(jnp.dot(xb, w_ref[:, :d], preferred_element_type=F32), q_ref,
               lambda x: rope(x) * (LOG2_E * HEAD_DIM ** -0.5))
    head_major(jnp.dot(xb, w_ref[:, d:2 * d], preferred_element_type=F32), k_ref, rope)
    head_major(jnp.dot(xb, w_ref[:, 2 * d:], preferred_element_type=F32), v_ref, lambda x: x)


def _ffn(h, mods, norm_g, w_gu, w_down, layer, which, n_rows, row_of_tile, final_g=None, h_tail=None,
         attn=None, qkv=None, cast_next=()):
    d = h.shape[1]
    tm = ROW_TILE
    k0 = 6 * which
    final = final_g is not None
    n_first = None
    in_specs, args = [pl.BlockSpec((tm, d), lambda t: (t, 0))], [h]
    if h_tail is not None:
        n_first = h.shape[0] // tm
        in_specs = [pl.BlockSpec((tm, d), lambda t: (jnp.minimum(t, n_first - 1), 0)),
                    pl.BlockSpec((tm, d), lambda t: (jnp.maximum(t - n_first, 0), 0))]
        args = [h, h_tail]
    head_major = pl.BlockSpec((N_HEADS, tm, V_DIM), lambda t: (0, t, 0))
    if attn is not None:
        o_hm, w_out = attn
        in_specs += [head_major, _mod_spec(layer, 5, d, row_of_tile), _stacked(w_out, (), single_buffer=True)]
        args += [o_hm, mods, w_out]
    in_specs += [
        _mod_spec(layer, k0, d, row_of_tile),
        _mod_spec(layer, k0 + 1, d, row_of_tile),
        _mod_spec(layer, k0 + 2, d, row_of_tile),
        _stacked(norm_g, (layer, 2 * which)),
        _stacked(w_gu, (), single_buffer=True),
        _stacked(w_down, (), single_buffer=True),
    ]
    args += [mods, mods, mods, norm_g, w_gu, w_down]
    if final:
        in_specs.append(pl.BlockSpec((1, d), lambda t: (0, 0)))
        args.append(final_g.reshape(1, d))
    out_specs = [pl.BlockSpec((tm, d), lambda t: (t, 0))]
    out_shape = [jax.ShapeDtypeStruct((n_rows, d), F32)]
    if qkv is not None:
        w_in, cos_t, sin_t, rope_block = qkv
        in_specs += [_mod_spec(layer, 3, d, row_of_tile), _mod_spec(layer, 4, d, row_of_tile),
                     _stacked(norm_g, (layer, 1)), _stacked(w_in, (), single_buffer=True),
                     pl.BlockSpec((tm, V_DIM), lambda t: (rope_block(t), 0)),
                     pl.BlockSpec((tm, V_DIM), lambda t: (rope_block(t), 0))]
        args += [mods, mods, norm_g, w_in, cos_t, sin_t]
        out_specs += [head_major] * 3
        out_shape += [jax.ShapeDtypeStruct((N_HEADS, n_rows, V_DIM), BF16)] * 3
    n_steps = n_rows // tm
    for w32, idx in cast_next:
        rows, cols = w32.shape[len(idx):]
        n_blk = max(n for n in range(1, n_steps + 1)
                    if rows % n == 0 and (rows // n) % V7X_BF16_SUBLANES == 0)
        blk = rows // n_blk
        in_specs.append(pl.BlockSpec((None,) * len(idx) + (blk, cols),
                                     lambda t, idx=idx, n_blk=n_blk: idx + (jnp.minimum(t, n_blk - 1), 0)))
        args.append(w32)
        out_specs.append(pl.BlockSpec((blk, cols), lambda t, n_blk=n_blk: (jnp.minimum(t, n_blk - 1), 0)))
        out_shape.append(jax.ShapeDtypeStruct((rows, cols), BF16))
    return pl.pallas_call(
        functools.partial(_ffn_kernel, n_first=n_first, attn=attn is not None, final=final,
                          qkv=qkv is not None, n_cast=len(cast_next)),
        grid=(n_steps,),
        in_specs=in_specs,
        out_specs=out_specs,
        out_shape=out_shape,
        compiler_params=_params(1),
        name="ffn",
    )(*args)


def _attn_kernel(q_ref, ql_ref, qc_ref, kl_ref, kc_ref, vl_ref, vc_ref, lam_ref, sg_ref, o_ref, bounded_scr, *,
                 lam_init, ctx_tile):
    n_heads, tq, _ = q_ref.shape
    lv = lam_ref[...]
    lam = (jnp.exp(jnp.sum(lv[0:1] * lv[1:2], axis=-1, keepdims=True))
           - jnp.exp(jnp.sum(lv[2:3] * lv[3:4], axis=-1, keepdims=True)) + lam_init)
    lane = lax.broadcasted_iota(jnp.int32, (tq, V_DIM), 1)
    zero = jnp.zeros((tq, V_DIM), BF16)

    gs_row = lax.broadcasted_iota(jnp.int32, (V_DIM, V_DIM), 0)
    gs_col = lax.broadcasted_iota(jnp.int32, (V_DIM, V_DIM), 1)
    group_sum = jnp.where(gs_row // HEAD_DIM == gs_col, 1.0, 0.0).astype(BF16)

    def max_sq_norm(*xs):
        return functools.reduce(jnp.maximum, [
            jnp.max(jnp.dot(x * x, group_sum, preferred_element_type=F32), axis=0, keepdims=True) for x in xs])

    @pl.when(pl.program_id(2) == 0)
    def _():
        bound_sq = jnp.max(functools.reduce(jnp.maximum, [
            max_sq_norm(ql_ref[hd], qc_ref[hd]) * max_sq_norm(kl_ref[hd], kc_ref[hd]) for hd in range(n_heads)]))
        bounded_scr[0] = (bound_sq * NORM_MARGIN <= MAX_UNSHIFTED_SCORE ** 2).astype(jnp.int32)

    scores_bounded = bounded_scr[0] == 1
    nt = (((1,), (1,)), ((), ()))

    def attend(latent_keys, subtract_max):
        c = kc_ref.shape[1]

        def scores(hd):
            q = q_ref[hd]
            qs = jnp.concatenate([jnp.where(lane < HEAD_DIM, q, zero),
                                  jnp.where(lane >= HEAD_DIM, q, zero)], axis=0)
            s = lax.dot_general(qs, kc_ref[hd], nt, preferred_element_type=F32)
            if latent_keys:
                s = jnp.concatenate([s, lax.dot_general(qs, kl_ref[hd], nt, preferred_element_type=F32)],
                                    axis=-1)
            return s

        all_scores = [scores(hd) for hd in range(n_heads)]
        for hd in range(n_heads):
            s = all_scores[hd]
            if subtract_max:
                s = s - jnp.max(s, axis=-1, keepdims=True)
            e = jnp.exp2(s)
            inv = 1.0 / jnp.sum(e, axis=-1, keepdims=True)
            w = (e[:tq] - e[tq:] * (lam * inv[tq:] / inv[:tq])).astype(BF16)
            o = jnp.dot(w[:, :c], vc_ref[hd], preferred_element_type=F32)
            if latent_keys:
                o = o + jnp.dot(w[:, c:], vl_ref[hd], preferred_element_type=F32)
            o = _rms(o * inv[:tq], sg_ref[...]) * (1.0 - lam_init)
            o_ref[hd] = o.astype(BF16)

    def attend_guarded(latent_keys):
        pl.when(scores_bounded)(lambda: attend(latent_keys, subtract_max=False))
        pl.when(jnp.logical_not(scores_bounded))(lambda: attend(latent_keys, subtract_max=True))

    if ctx_tile:
        is_ctx = pl.program_id(2) == 0
        pl.when(is_ctx)(lambda: attend_guarded(False))
        pl.when(jnp.logical_not(is_ctx))(lambda: attend_guarded(True))
    else:
        attend_guarded(True)


def _attention(q, k, v, lam_vecs, subln_g, lam_init, j, dims, context_queries):
    b, s, c = dims
    tq = ATTN_TILE
    assert c == tq
    lat_per_b = s // tq
    ctx_blocks_from = b * s // c
    if context_queries:
        n_qt, out_rows = lat_per_b + 1, b * (s + c)
        q_block = lambda bb, jq: jnp.where(jq == 0, ctx_blocks_from + bb, bb * lat_per_b + jq - 1)
    else:
        n_qt, out_rows = lat_per_b, b * s
        q_block = lambda bb, jq: bb * lat_per_b + jq
    hb = HEADS_PER_STEP
    lat_spec = pl.BlockSpec((hb, s, V_DIM), lambda bb, hg, jq: (hg, bb, 0))
    ctx_spec = pl.BlockSpec((hb, c, V_DIM), lambda bb, hg, jq: (hg, ctx_blocks_from + bb, 0))
    tile_spec = pl.BlockSpec((hb, tq, V_DIM), lambda bb, hg, jq: (hg, q_block(bb, jq), 0))
    return pl.pallas_call(
        functools.partial(_attn_kernel, lam_init=lam_init, ctx_tile=context_queries),
        grid=(b, N_HEADS // hb, n_qt),
        in_specs=[tile_spec, lat_spec, ctx_spec, lat_spec, ctx_spec, lat_spec, ctx_spec,
                  _stacked(lam_vecs, (j,)), _stacked(subln_g, (j,))],
        out_specs=tile_spec,
        out_shape=jax.ShapeDtypeStruct((N_HEADS, out_rows, V_DIM), BF16),
        scratch_shapes=[pltpu.SMEM((1,), jnp.int32)],
        compiler_params=_params(3),
        name="attention",
    )(q, q, q, k, k, v, v, lam_vecs, subln_g)


def _sg_kernel(h_ref, sh_ref, sc_ref, gt_ref, g_ref, win_ref, lng_ref, lnb_ref, ws_ref, bs_ref, wout_ref,
               o_ref):
    tm = h_ref.shape[0]
    e = wout_ref.shape[0]
    gd = e // SG_GROUPS
    h = h_ref[...]
    xb = _modulated_bf16(h, g_ref[...], sh_ref[...], sc_ref[...])
    u = _gelu_tanh(jnp.dot(xb, win_ref[:, :e], preferred_element_type=F32))
    v = _gelu_tanh(jnp.dot(xb, win_ref[:, e:], preferred_element_type=F32))
    mu = jnp.mean(v, axis=-1, keepdims=True)
    vc = v - mu
    vn = vc * lax.rsqrt(jnp.mean(vc * vc, axis=-1, keepdims=True) + LN_EPS)
    vb = (vn * lng_ref[...] + lnb_ref[...]).astype(BF16)
    bias = bs_ref[...]
    rows = []
    for n in range(tm // CHUNK):
        r = slice(n * CHUNK, (n + 1) * CHUNK)
        cols = [jnp.dot(ws_ref[gi], vb[r, gi * gd:(gi + 1) * gd], preferred_element_type=F32)
                for gi in range(SG_GROUPS)]
        rows.append(jnp.concatenate(cols, axis=-1) + bias)
    mixed = jnp.concatenate(rows, axis=0)
    y = jnp.dot((u * mixed).astype(BF16), wout_ref[...], preferred_element_type=F32)
    o_ref[...] = h + gt_ref[...] * y


def _sg_mixer(h, mods, norm_g, w_in, ln_g, ln_b, w_s, bias_full, w_out, layer, j, n_rows, row_of_tile):
    d = h.shape[1]
    tm = SG_ROW_TILE
    return pl.pallas_call(
        _sg_kernel,
        grid=(n_rows // tm,),
        in_specs=[
            pl.BlockSpec((tm, d), lambda t: (t, 0)),
            _mod_spec(layer, 3, d, row_of_tile),
            _mod_spec(layer, 4, d, row_of_tile),
            _mod_spec(layer, 5, d, row_of_tile),
            _stacked(norm_g, (layer, 1)),
            _stacked(w_in, (), single_buffer=True),
            _stacked(ln_g, (j,)),
            _stacked(ln_b, (j,)),
            _stacked(w_s, (j,)),
            _stacked(bias_full, (j,)),
            _stacked(w_out, (), single_buffer=True),
        ],
        out_specs=pl.BlockSpec((tm, d), lambda t: (t, 0)),
        out_shape=jax.ShapeDtypeStruct((n_rows, d), F32),
        compiler_params=_params(1),
        name="sg_mixer",
    )(h, mods, mods, mods, norm_g, w_in, ln_g, ln_b, w_s, bias_full, w_out)


def _rope_tables(s, tm):
    rows_n = s // GRID_W
    row = jnp.repeat(jnp.arange(rows_n), GRID_W)
    col = jnp.tile(jnp.arange(GRID_W), rows_n)
    n_freq = HEAD_DIM // 4
    inv = ROPE_THETA ** (-jnp.arange(n_freq, dtype=F32) / n_freq)
    pos = jnp.stack([row, col], axis=-1).astype(F32)
    ang = pos[:, :, None] * inv
    cos, sin = jnp.cos(ang), jnp.sin(ang)
    cos_l = jnp.broadcast_to(cos[:, None, :, None, :], (s, 2, 2, 2, n_freq)).reshape(s, V_DIM)
    sign = jnp.array([-1.0, 1.0], F32)[None, None, None, :, None]
    sin_l = jnp.broadcast_to(sin[:, None, :, None, :] * sign, (s, 2, 2, 2, n_freq)).reshape(s, V_DIM)
    cos_t = jnp.concatenate([cos_l, jnp.ones((tm, V_DIM), F32)], axis=0)
    sin_t = jnp.concatenate([sin_l, jnp.zeros((tm, V_DIM), F32)], axis=0)
    return cos_t, sin_t


def kernel(x, c, ctx, c_ctx, w_mod, b_mod, norm_g, w_ffn_gu, w_ffn_down, da_w_in, da_w_out, da_lambda,
           da_subln_g, sg_w_in, sg_ln_g, sg_ln_b, sg_w_s, sg_b_s, sg_w_out, final_g):
    b, s, d = x.shape
    cl = ctx.shape[1]
    depth = w_mod.shape[0]
    e = sg_w_out.shape[1]
    assert cl == ATTN_TILE and s % ROW_TILE == 0 and s % GRID_W == 0 and b < MOD_ROWS
    assert (b * cl) % ROW_TILE == 0 and cl % CHUNK == 0
    assert s % SG_ROW_TILE == 0 and (b * cl) % SG_ROW_TILE == 0 and SG_ROW_TILE % CHUNK == 0
    dims = (b, s, cl)
    n_lat, n_all = b * s, b * (s + cl)
    tiles_per_b = s // ROW_TILE

    def row_of_tile(t):
        return jnp.minimum(t // tiles_per_b, b)

    def rope_block(t):
        return jnp.where(t < n_lat // ROW_TILE, t % tiles_per_b, tiles_per_b)

    c_all = jnp.zeros((MOD_ROWS, d), F32).at[:b].set(c).at[b].set(c_ctx)
    mods = _mod_table(c_all, w_mod, b_mod).reshape(depth, MOD_ROWS, N_MOD, 1, d)
    cos_t, sin_t = _rope_tables(s, ROW_TILE)

    norm_g4 = norm_g.reshape(depth, 3, 1, d)
    ffn_w = [w_ffn_gu[0, 0].astype(BF16), w_ffn_down[0, 0].astype(BF16)]

    def mixer_weights(i):
        w_in, w_out = (da_w_in, da_w_out) if i % N_MIXERS == 0 else (sg_w_in, sg_w_out)
        return (w_in, (i // N_MIXERS,)), (w_out, (i // N_MIXERS,))

    mix_w = [w[idx].astype(BF16) for w, idx in mixer_weights(0)]
    g_ws = sg_w_s.astype(BF16)
    subln_g = da_subln_g.reshape(-1, 1, V_DIM)
    ln_g, ln_b = sg_ln_g.reshape(-1, 1, e), sg_ln_b.reshape(-1, 1, e)
    bias_full = jnp.repeat(jnp.swapaxes(sg_b_s, 1, 2), e // SG_GROUPS, axis=2)

    last_ctx_layer = max(i for i in range(depth) if i % N_MIXERS == 0)
    h = x.reshape(n_lat, d)
    for i in range(depth):
        j = i // N_MIXERS
        attention_layer = i % N_MIXERS == 0
        rows_in = n_all if i <= last_ctx_layer else n_lat
        rows_out = n_all if i < last_ctx_layer else n_lat
        h_tail = ctx.reshape(b * cl, d) if i == 0 and rows_in == n_all else None
        assert not attention_layer or rows_in == n_all
        h, *rest = _ffn(h, mods, norm_g4, *ffn_w, i, 0, rows_in, row_of_tile, h_tail=h_tail,
                        qkv=(mix_w[0], cos_t, sin_t, rope_block) if attention_layer else None,
                        cast_next=((w_ffn_gu, (i, 1)), (w_ffn_down, (i, 1))))
        ffn_w = rest[-2:]
        attn = None
        if attention_layer:
            q, k, v = rest[:3]
            lam_init = 0.8 - 0.6 * math.exp(-0.3 * i)
            o = _attention(q, k, v, da_lambda, subln_g, lam_init, j, dims,
                           context_queries=i < last_ctx_layer)
            attn = (o, mix_w[1])
        else:
            h = _sg_mixer(h, mods, norm_g4, mix_w[0], ln_g, ln_b, g_ws, bias_full, mix_w[1], i, j, rows_out,
                          lambda t: jnp.minimum(t // (s // SG_ROW_TILE), b))
        last = i == depth - 1
        cast_next = () if last else (((w_ffn_gu, (i + 1, 0)), (w_ffn_down, (i + 1, 0)))
                                     + mixer_weights(i + 1))
        h, *cast = _ffn(h, mods, norm_g4, *ffn_w, i, 1, rows_out, row_of_tile,
                        final_g=final_g if last else None, attn=attn, cast_next=cast_next)
        ffn_w, mix_w = cast[:2], cast[2:]
    return h.reshape(b, s, d)
```

```python
import functools
import math

import jax
import jax.numpy as jnp
from jax import lax
from jax.experimental import pallas as pl
from jax.experimental.pallas import tpu as pltpu

F32 = jnp.float32
BF16 = jnp.bfloat16

GRID_W = 64
N_MIXERS = 2
N_HEADS = 8
HEAD_DIM = 64
V_DIM = 2 * HEAD_DIM
ROPE_THETA = 10000.0
CHUNK = 128
SG_GROUPS = 8
N_MOD = 9
RMS_EPS = 1e-6
LN_EPS = 1e-5

V7X_LANES = 128
V7X_BF16_SUBLANES = 16
V7X_VMEM_BYTES = 64 * 1024 * 1024
VMEM_LIMIT_BYTES = V7X_VMEM_BYTES - 8 * 1024 * 1024

MOD_ROWS = 16
ATTN_TILE = 256
ROW_TILE = 512
WIDE_ROW_TILE = 1024
HEADS_PER_STEP = 4
LOG2_E = math.log2(math.e)
MAX_UNSHIFTED_SCORE = 40.0
NORM_MARGIN = 1.05


def _params(n_axes):
    return pltpu.CompilerParams(dimension_semantics=("arbitrary",) * n_axes,
                                vmem_limit_bytes=VMEM_LIMIT_BYTES)


def _rms(x, g):
    y = x * lax.rsqrt(jnp.mean(x * x, axis=-1, keepdims=True) + RMS_EPS)
    return y * g


def _modulated_bf16(h, g, shift, scale):
    r = lax.rsqrt(jnp.mean(h * h, axis=-1, keepdims=True) + RMS_EPS)
    return ((h * r) * (g * (1.0 + scale)) + shift).astype(BF16)


def _gelu_tanh(x):
    c1 = -2.0 * LOG2_E * math.sqrt(2.0 / math.pi)
    return x / (1.0 + jnp.exp2(x * (c1 + (0.044715 * c1) * (x * x))))


def _stacked(arr, idx, single_buffer=False):
    tail = arr.shape[len(idx):]
    index = tuple(idx) + (0,) * len(tail)
    kwargs = dict(pipeline_mode=pl.Buffered(1)) if single_buffer else {}
    return pl.BlockSpec((None,) * len(idx) + tail, lambda *_: index, **kwargs)


def _mod_spec(layer, k, d, row_of_tile):
    return pl.BlockSpec((None, None, None, 1, d), lambda t: (layer, row_of_tile(t), k, 0, 0))


def _mod_kernel(c_ref, w_ref, b_ref, o_ref):
    sc = jax.nn.silu(c_ref[...]).astype(BF16)
    o_ref[...] = jnp.dot(sc, w_ref[...].astype(BF16), preferred_element_type=F32) + b_ref[...]


def _mod_table(c_all, w_mod, b_mod):
    depth, d, nd = w_mod.shape
    tn = d
    return pl.pallas_call(
        _mod_kernel,
        grid=(depth, nd // tn),
        in_specs=[
            pl.BlockSpec((MOD_ROWS, d), lambda i, n: (0, 0)),
            pl.BlockSpec((None, d, tn), lambda i, n: (i, 0, n)),
            pl.BlockSpec((None, 1, tn), lambda i, n: (i, 0, n)),
        ],
        out_specs=pl.BlockSpec((None, MOD_ROWS, tn), lambda i, n: (i, 0, n)),
        out_shape=jax.ShapeDtypeStruct((depth, MOD_ROWS, nd), F32),
        compiler_params=_params(2),
        name="mod_table",
    )(c_all, w_mod, b_mod.reshape(depth, 1, nd))


def _ffn_kernel(*refs, n_first, attn, final, qkv, n_cast):
    refs = list(refs)
    cast_out = [refs.pop() for _ in range(n_cast)][::-1]
    qkv_out = [refs.pop() for _ in range(3 if qkv else 0)][::-1]
    o_ref = refs.pop()
    cast_in = [refs.pop() for _ in range(n_cast)][::-1]
    for src_ref, dst_ref in zip(cast_in, cast_out):
        dst_ref[...] = src_ref[...].astype(BF16)
    qkv_in = [refs.pop() for _ in range(6 if qkv else 0)][::-1]
    fg_ref = refs.pop() if final else None
    h_ref = refs.pop(0)
    h = h_ref[...]
    if n_first is not None:
        h = jnp.where(pl.program_id(0) < n_first, h, refs.pop(0)[...])
    if attn:
        attn_ref, mixer_gate_ref, wout_ref = refs.pop(0), refs.pop(0), refs.pop(0)
        heads = jnp.concatenate([attn_ref[hd] for hd in range(N_HEADS)], axis=-1)
        h = h + mixer_gate_ref[...] * jnp.dot(heads, wout_ref[...], preferred_element_type=F32)
    sh_ref, sc_ref, gt_ref, g_ref, wgu_ref, wd_ref = refs
    d_ff = wd_ref.shape[0]
    xb = _modulated_bf16(h, g_ref[...], sh_ref[...], sc_ref[...])
    half = h.shape[0] // 2
    ys = []
    for part in range(2):
        xp = xb[part * half:(part + 1) * half]
        a = jnp.dot(xp, wgu_ref[:, :d_ff], preferred_element_type=F32)
        u = jnp.dot(xp, wgu_ref[:, d_ff:], preferred_element_type=F32)
        act = (jax.nn.silu(a) * u).astype(BF16)
        ys.append(jnp.dot(act, wd_ref[...], preferred_element_type=F32))
    y = jnp.concatenate(ys, axis=0)
    out = h + (0.5 * gt_ref[...]) * y
    if final:
        out = _rms(out, fg_ref[...])
    o_ref[...] = out
    if qkv:
        msh_ref, msc_ref, mg_ref, win_ref, cos_ref, sin_ref = qkv_in
        xq = _modulated_bf16(out, mg_ref[...], msh_ref[...], msc_ref[...])
        _write_qkv(xq, win_ref, cos_ref[...], sin_ref[...], *qkv_out)


def _write_qkv(xb, w_ref, cos, sin, q_ref, k_ref, v_ref):
    d = xb.shape[1]
    lane = lax.broadcasted_iota(jnp.int32, cos.shape, 1)
    first_half = (lane % (HEAD_DIM // 2)) < (HEAD_DIM // 4)

    def rope(x):
        partner = jnp.where(first_half,
                            pltpu.roll(x, V7X_LANES - HEAD_DIM // 4, axis=1),
                            pltpu.roll(x, HEAD_DIM // 4, axis=1))
        return x * cos + partner * sin

    def head_major(x, ref, fn):
        for hd in range(N_HEADS):
            ref[hd] = fn(x[:, hd * V_DIM:(hd + 1) * V_DIM]).astype(BF16)

    head_major(jnp.dot(xb, w_ref[:, :d], preferred_element_type=F32), q_ref,
               lambda x: rope(x) * (LOG2_E * HEAD_DIM ** -0.5))
    head_major(jnp.dot(xb, w_ref[:, d:2 * d], preferred_element_type=F32), k_ref, rope)
    head_major(jnp.dot(xb, w_ref[:, 2 * d:], preferred_element_type=F32), v_ref, lambda x: x)


def _ffn(h, mods, norm_g, w_gu, w_down, layer, which, n_rows, row_of_tile, final_g=None, h_tail=None,
         attn=None, qkv=None, cast_next=(), tm=ROW_TILE):
    d = h.shape[1]
    k0 = 6 * which
    final = final_g is not None
    n_first = None
    in_specs, args = [pl.BlockSpec((tm, d), lambda t: (t, 0))], [h]
    if h_tail is not None:
        n_first = h.shape[0] // tm
        in_specs = [pl.BlockSpec((tm, d), lambda t: (jnp.minimum(t, n_first - 1), 0)),
                    pl.BlockSpec((tm, d), lambda t: (jnp.maximum(t - n_first, 0), 0))]
        args = [h, h_tail]
    head_major = pl.BlockSpec((N_HEADS, tm, V_DIM), lambda t: (0, t, 0))
    if attn is not None:
        o_hm, w_out = attn
        in_specs += [head_major, _mod_spec(layer, 5, d, row_of_tile), _stacked(w_out, (), single_buffer=True)]
        args += [o_hm, mods, w_out]
    in_specs += [
        _mod_spec(layer, k0, d, row_of_tile),
        _mod_spec(layer, k0 + 1, d, row_of_tile),
        _mod_spec(layer, k0 + 2, d, row_of_tile),
        _stacked(norm_g, (layer, 2 * which)),
        _stacked(w_gu, (), single_buffer=True),
        _stacked(w_down, (), single_buffer=True),
    ]
    args += [mods, mods, mods, norm_g, w_gu, w_down]
    if final:
        in_specs.append(pl.BlockSpec((1, d), lambda t: (0, 0)))
        args.append(final_g.reshape(1, d))
    out_specs = [pl.BlockSpec((tm, d), lambda t: (t, 0))]
    out_shape = [jax.ShapeDtypeStruct((n_rows, d), F32)]
    if qkv is not None:
        w_in, cos_t, sin_t, rope_block = qkv
        in_specs += [_mod_spec(layer, 3, d, row_of_tile), _mod_spec(layer, 4, d, row_of_tile),
                     _stacked(norm_g, (layer, 1)), _stacked(w_in, (), single_buffer=True),
                     pl.BlockSpec((tm, V_DIM), lambda t: (rope_block(t), 0)),
                     pl.BlockSpec((tm, V_DIM), lambda t: (rope_block(t), 0))]
        args += [mods, mods, norm_g, w_in, cos_t, sin_t]
        out_specs += [head_major] * 3
        out_shape += [jax.ShapeDtypeStruct((N_HEADS, n_rows, V_DIM), BF16)] * 3
    n_steps = n_rows // tm
    for w32, idx in cast_next:
        rows, cols = w32.shape[len(idx):]
        n_blk = max(n for n in range(1, n_steps + 1)
                    if rows % n == 0 and (rows // n) % V7X_BF16_SUBLANES == 0)
        blk = rows // n_blk
        in_specs.append(pl.BlockSpec((None,) * len(idx) + (blk, cols),
                                     lambda t, idx=idx, n_blk=n_blk: idx + (jnp.minimum(t, n_blk - 1), 0)))
        args.append(w32)
        out_specs.append(pl.BlockSpec((blk, cols), lambda t, n_blk=n_blk: (jnp.minimum(t, n_blk - 1), 0)))
        out_shape.append(jax.ShapeDtypeStruct((rows, cols), BF16))
    return pl.pallas_call(
        functools.partial(_ffn_kernel, n_first=n_first, attn=attn is not None, final=final,
                          qkv=qkv is not None, n_cast=len(cast_next)),
        grid=(n_steps,),
        in_specs=in_specs,
        out_specs=out_specs,
        out_shape=out_shape,
        compiler_params=_params(1),
        name="ffn",
    )(*args)


def _attn_kernel(q_ref, ql_ref, qc_ref, kl_ref, kc_ref, vl_ref, vc_ref, lam_ref, sg_ref, o_ref, bounded_scr, *,
                 lam_init, ctx_tile):
    n_heads, tq, _ = q_ref.shape
    lv = lam_ref[...]
    lam = (jnp.exp(jnp.sum(lv[0:1] * lv[1:2], axis=-1, keepdims=True))
           - jnp.exp(jnp.sum(lv[2:3] * lv[3:4], axis=-1, keepdims=True)) + lam_init)
    lane = lax.broadcasted_iota(jnp.int32, (tq, V_DIM), 1)
    zero = jnp.zeros((tq, V_DIM), BF16)

    gs_row = lax.broadcasted_iota(jnp.int32, (V_DIM, V_DIM), 0)
    gs_col = lax.broadcasted_iota(jnp.int32, (V_DIM, V_DIM), 1)
    group_sum = jnp.where(gs_row // HEAD_DIM == gs_col, 1.0, 0.0).astype(BF16)

    def max_sq_norm(*xs):
        return functools.reduce(jnp.maximum, [
            jnp.max(jnp.dot(x * x, group_sum, preferred_element_type=F32), axis=0, keepdims=True) for x in xs])

    @pl.when(pl.program_id(2) == 0)
    def _():
        bound_sq = jnp.max(functools.reduce(jnp.maximum, [
            max_sq_norm(ql_ref[hd], qc_ref[hd]) * max_sq_norm(kl_ref[hd], kc_ref[hd]) for hd in range(n_heads)]))
        bounded_scr[0] = (bound_sq * NORM_MARGIN <= MAX_UNSHIFTED_SCORE ** 2).astype(jnp.int32)

    scores_bounded = bounded_scr[0] == 1
    nt = (((1,), (1,)), ((), ()))

    def attend(latent_keys, subtract_max):
        c = kc_ref.shape[1]

        def scores(hd):
            q = q_ref[hd]
            qs = jnp.concatenate([jnp.where(lane < HEAD_DIM, q, zero),
                                  jnp.where(lane >= HEAD_DIM, q, zero)], axis=0)
            s = lax.dot_general(qs, kc_ref[hd], nt, preferred_element_type=F32)
            if latent_keys:
                s = jnp.concatenate([s, lax.dot_general(qs, kl_ref[hd], nt, preferred_element_type=F32)],
                                    axis=-1)
            return s

        all_scores = [scores(hd) for hd in range(n_heads)]
        for hd in range(n_heads):
            s = all_scores[hd]
            if subtract_max:
                s = s - jnp.max(s, axis=-1, keepdims=True)
            e = jnp.exp2(s)
            inv = 1.0 / jnp.sum(e, axis=-1, keepdims=True)
            w = (e[:tq] - e[tq:] * (lam * inv[tq:] / inv[:tq])).astype(BF16)
            o = jnp.dot(w[:, :c], vc_ref[hd], preferred_element_type=F32)
            if latent_keys:
                o = o + jnp.dot(w[:, c:], vl_ref[hd], preferred_element_type=F32)
            o = _rms(o * inv[:tq], sg_ref[...]) * (1.0 - lam_init)
            o_ref[hd] = o.astype(BF16)

    def attend_guarded(latent_keys):
        pl.when(scores_bounded)(lambda: attend(latent_keys, subtract_max=False))
        pl.when(jnp.logical_not(scores_bounded))(lambda: attend(latent_keys, subtract_max=True))

    if ctx_tile:
        is_ctx = pl.program_id(2) == 0
        pl.when(is_ctx)(lambda: attend_guarded(False))
        pl.when(jnp.logical_not(is_ctx))(lambda: attend_guarded(True))
    else:
        attend_guarded(True)


def _attention(q, k, v, lam_vecs, subln_g, lam_init, j, dims, context_queries):
    b, s, c = dims
    tq = ATTN_TILE
    assert c == tq
    lat_per_b = s // tq
    ctx_blocks_from = b * s // c
    if context_queries:
        n_qt, out_rows = lat_per_b + 1, b * (s + c)
        q_block = lambda bb, jq: jnp.where(jq == 0, ctx_blocks_from + bb, bb * lat_per_b + jq - 1)
    else:
        n_qt, out_rows = lat_per_b, b * s
        q_block = lambda bb, jq: bb * lat_per_b + jq
    hb = HEADS_PER_STEP
    lat_spec = pl.BlockSpec((hb, s, V_DIM), lambda bb, hg, jq: (hg, bb, 0))
    ctx_spec = pl.BlockSpec((hb, c, V_DIM), lambda bb, hg, jq: (hg, ctx_blocks_from + bb, 0))
    tile_spec = pl.BlockSpec((hb, tq, V_DIM), lambda bb, hg, jq: (hg, q_block(bb, jq), 0))
    return pl.pallas_call(
        functools.partial(_attn_kernel, lam_init=lam_init, ctx_tile=context_queries),
        grid=(b, N_HEADS // hb, n_qt),
        in_specs=[tile_spec, lat_spec, ctx_spec, lat_spec, ctx_spec, lat_spec, ctx_spec,
                  _stacked(lam_vecs, (j,)), _stacked(subln_g, (j,))],
        out_specs=tile_spec,
        out_shape=jax.ShapeDtypeStruct((N_HEADS, out_rows, V_DIM), BF16),
        scratch_shapes=[pltpu.SMEM((1,), jnp.int32)],
        compiler_params=_params(3),
        name="attention",
    )(q, q, q, k, k, v, v, lam_vecs, subln_g)


def _sg_kernel(h_ref, sh_ref, sc_ref, gt_ref, g_ref, win_ref, lng_ref, lnb_ref, ws_ref, bs_ref, wout_ref,
               o_ref):
    tm = h_ref.shape[0]
    e = wout_ref.shape[0]
    gd = e // SG_GROUPS
    h = h_ref[...]
    xb = _modulated_bf16(h, g_ref[...], sh_ref[...], sc_ref[...])
    u = _gelu_tanh(jnp.dot(xb, win_ref[:, :e], preferred_element_type=F32))
    v = _gelu_tanh(jnp.dot(xb, win_ref[:, e:], preferred_element_type=F32))
    mu = jnp.mean(v, axis=-1, keepdims=True)
    vc = v - mu
    vn = vc * lax.rsqrt(jnp.mean(vc * vc, axis=-1, keepdims=True) + LN_EPS)
    vb = (vn * lng_ref[...] + lnb_ref[...]).astype(BF16)
    bias = bs_ref[...]
    rows = []
    for n in range(tm // CHUNK):
        r = slice(n * CHUNK, (n + 1) * CHUNK)
        cols = [jnp.dot(ws_ref[gi], vb[r, gi * gd:(gi + 1) * gd], preferred_element_type=F32)
                for gi in range(SG_GROUPS)]
        rows.append(jnp.concatenate(cols, axis=-1) + bias)
    mixed = jnp.concatenate(rows, axis=0)
    y = jnp.dot((u * mixed).astype(BF16), wout_ref[...], preferred_element_type=F32)
    o_ref[...] = h + gt_ref[...] * y


def _sg_mixer(h, mods, norm_g, w_in, ln_g, ln_b, w_s, bias_full, w_out, layer, j, n_rows, row_of_tile):
    d = h.shape[1]
    tm = ROW_TILE
    return pl.pallas_call(
        _sg_kernel,
        grid=(n_rows // tm,),
        in_specs=[
            pl.BlockSpec((tm, d), lambda t: (t, 0)),
            _mod_spec(layer, 3, d, row_of_tile),
            _mod_spec(layer, 4, d, row_of_tile),
            _mod_spec(layer, 5, d, row_of_tile),
            _stacked(norm_g, (layer, 1)),
            _stacked(w_in, (), single_buffer=True),
            _stacked(ln_g, (j,)),
            _stacked(ln_b, (j,)),
            _stacked(w_s, (j,)),
            _stacked(bias_full, (j,)),
            _stacked(w_out, (), single_buffer=True),
        ],
        out_specs=pl.BlockSpec((tm, d), lambda t: (t, 0)),
        out_shape=jax.ShapeDtypeStruct((n_rows, d), F32),
        compiler_params=_params(1),
        name="sg_mixer",
    )(h, mods, mods, mods, norm_g, w_in, ln_g, ln_b, w_s, bias_full, w_out)


def _rope_tables(s, tm):
    rows_n = s // GRID_W
    row = jnp.repeat(jnp.arange(rows_n), GRID_W)
    col = jnp.tile(jnp.arange(GRID_W), rows_n)
    n_freq = HEAD_DIM // 4
    inv = ROPE_THETA ** (-jnp.arange(n_freq, dtype=F32) / n_freq)
    pos = jnp.stack([row, col], axis=-1).astype(F32)
    ang = pos[:, :, None] * inv
    cos, sin = jnp.cos(ang), jnp.sin(ang)
    cos_l = jnp.broadcast_to(cos[:, None, :, None, :], (s, 2, 2, 2, n_freq)).reshape(s, V_DIM)
    sign = jnp.array([-1.0, 1.0], F32)[None, None, None, :, None]
    sin_l = jnp.broadcast_to(sin[:, None, :, None, :] * sign, (s, 2, 2, 2, n_freq)).reshape(s, V_DIM)
    cos_t = jnp.concatenate([cos_l, jnp.ones((tm, V_DIM), F32)], axis=0)
    sin_t = jnp.concatenate([sin_l, jnp.zeros((tm, V_DIM), F32)], axis=0)
    return cos_t, sin_t


def kernel(x, c, ctx, c_ctx, w_mod, b_mod, norm_g, w_ffn_gu, w_ffn_down, da_w_in, da_w_out, da_lambda,
           da_subln_g, sg_w_in, sg_ln_g, sg_ln_b, sg_w_s, sg_b_s, sg_w_out, final_g):
    b, s, d = x.shape
    cl = ctx.shape[1]
    depth = w_mod.shape[0]
    e = sg_w_out.shape[1]
    assert cl == ATTN_TILE and s % ROW_TILE == 0 and s % GRID_W == 0 and b < MOD_ROWS
    assert (b * cl) % ROW_TILE == 0 and ROW_TILE % CHUNK == 0 and cl % CHUNK == 0
    assert s % WIDE_ROW_TILE == 0 and (b * cl) % WIDE_ROW_TILE == 0
    dims = (b, s, cl)
    n_lat, n_all = b * s, b * (s + cl)
    tiles_per_b = s // ROW_TILE

    def row_of_tile(t):
        return jnp.minimum(t // tiles_per_b, b)

    def row_of(tm):
        return lambda t: jnp.minimum(t // (s // tm), b)

    def rope_block(t):
        return jnp.where(t < n_lat // ROW_TILE, t % tiles_per_b, tiles_per_b)

    c_all = jnp.zeros((MOD_ROWS, d), F32).at[:b].set(c).at[b].set(c_ctx)
    mods = _mod_table(c_all, w_mod, b_mod).reshape(depth, MOD_ROWS, N_MOD, 1, d)
    cos_t, sin_t = _rope_tables(s, ROW_TILE)

    norm_g4 = norm_g.reshape(depth, 3, 1, d)
    ffn_w = [w_ffn_gu[0, 0].astype(BF16), w_ffn_down[0, 0].astype(BF16)]

    def mixer_weights(i):
        w_in, w_out = (da_w_in, da_w_out) if i % N_MIXERS == 0 else (sg_w_in, sg_w_out)
        return (w_in, (i // N_MIXERS,)), (w_out, (i // N_MIXERS,))

    mix_w = [w[idx].astype(BF16) for w, idx in mixer_weights(0)]
    g_ws = sg_w_s.astype(BF16)
    subln_g = da_subln_g.reshape(-1, 1, V_DIM)
    ln_g, ln_b = sg_ln_g.reshape(-1, 1, e), sg_ln_b.reshape(-1, 1, e)
    bias_full = jnp.repeat(jnp.swapaxes(sg_b_s, 1, 2), e // SG_GROUPS, axis=2)

    last_ctx_layer = max(i for i in range(depth) if i % N_MIXERS == 0)
    h = x.reshape(n_lat, d)
    for i in range(depth):
        j = i // N_MIXERS
        attention_layer = i % N_MIXERS == 0
        rows_in = n_all if i <= last_ctx_layer else n_lat
        rows_out = n_all if i < last_ctx_layer else n_lat
        h_tail = ctx.reshape(b * cl, d) if i == 0 and rows_in == n_all else None
        assert not attention_layer or rows_in == n_all
        tm1 = ROW_TILE if attention_layer else WIDE_ROW_TILE
        h, *rest = _ffn(h, mods, norm_g4, *ffn_w, i, 0, rows_in, row_of(tm1), h_tail=h_tail,
                        qkv=(mix_w[0], cos_t, sin_t, rope_block) if attention_layer else None,
                        cast_next=((w_ffn_gu, (i, 1)), (w_ffn_down, (i, 1))), tm=tm1)
        ffn_w = rest[-2:]
        attn = None
        if attention_layer:
            q, k, v = rest[:3]
            lam_init = 0.8 - 0.6 * math.exp(-0.3 * i)
            o = _attention(q, k, v, da_lambda, subln_g, lam_init, j, dims,
                           context_queries=i < last_ctx_layer)
            attn = (o, mix_w[1])
        else:
            h = _sg_mixer(h, mods, norm_g4, mix_w[0], ln_g, ln_b, g_ws, bias_full, mix_w[1], i, j, rows_out,
                          row_of_tile)
        last = i == depth - 1
        cast_next = () if last else (((w_ffn_gu, (i + 1, 0)), (w_ffn_down, (i + 1, 0)))
                                     + mixer_weights(i + 1))
        h, *cast = _ffn(h, mods, norm_g4, *ffn_w, i, 1, rows_out, row_of(tm1),
                        final_g=final_g if last else None, attn=attn, cast_next=cast_next, tm=tm1)
        ffn_w, mix_w = cast[:2], cast[2:]
    return h.reshape(b, s, d)
```
